```python
import math
import jax, jax.numpy as jnp
from jax import lax
import numpy as np

D_MODEL = 2048
BATCH = 4
SEQ = 2048
DEPTH = 2
DEC_BATCH = 128
DEC_SEQ = 1
PAST_LEN = 2048
PAGE_SIZE = 128

N_A = DEPTH // 2
N_B = DEPTH - N_A
RET_HEADS = D_MODEL // 256
RET_DK = D_MODEL // RET_HEADS
RET_DV = 2 * RET_DK
RET_CHUNK = 128
RET_THETA = 10000.0
DIFF_HEADS = D_MODEL // 256
DIFF_DH = D_MODEL // (2 * DIFF_HEADS)
DIFF_DV = 2 * DIFF_DH
ROPE_DIM = DIFF_DH // 4
ROPE_THETA = 500000.0
Q_BLOCK = 128
D_FF = ((8 * D_MODEL // 3 + 127) // 128) * 128
CONV_W = 3
NORM_EPS = 1e-6

kernel_name = 'yoco_retention_diffattn_convffn_step'


def _rmsnorm(x, gain=None):
    xf = x.astype(jnp.float32)
    y = xf * lax.rsqrt(jnp.mean(xf * xf, axis=-1, keepdims=True) + NORM_EPS)
    if gain is not None:
        y = y * gain.astype(jnp.float32)
    return y.astype(x.dtype)


def _modulation(c, w, b, n):
    return jnp.split(jax.nn.silu(c) @ w + b, n, axis=-1)


def _ada_norm(x, gain, shift, scale):
    return _rmsnorm(x, gain) * (1.0 + scale[:, None, :]) + shift[:, None, :]


def _rope(x, pos, rot_dim, theta):
    half = rot_dim // 2
    inv = jnp.float32(theta) ** (-jnp.arange(half, dtype=jnp.float32) / half)
    ang = pos.astype(jnp.float32)[:, None] * inv[None, :]
    shape = (1, x.shape[1]) + (1,) * (x.ndim - 3) + (half,)
    cos = jnp.cos(ang).reshape(shape)
    sin = jnp.sin(ang).reshape(shape)
    xr = x[..., :rot_dim].astype(jnp.float32)
    x1, x2 = xr[..., :half], xr[..., half:]
    rot = jnp.concatenate([x1 * cos - x2 * sin, x2 * cos + x1 * sin], axis=-1).astype(x.dtype)
    return jnp.concatenate([rot, x[..., rot_dim:]], axis=-1)


def _retention_scan(q, k, v, state, chunk):
    b, l, nh, _ = q.shape
    dv = v.shape[-1]
    n = l // chunk
    log_g = jnp.log1p(-jnp.exp2(-5.0 - jnp.arange(nh, dtype=jnp.float32)))
    idx = jnp.arange(chunk, dtype=jnp.float32)
    rel = idx[:, None] - idx[None, :]
    decay = jnp.where(rel >= 0, jnp.exp(jnp.maximum(rel, 0.0)[None] * log_g[:, None, None]), 0.0)
    q_dec = jnp.exp((idx[:, None] + 1.0) * log_g[None, :])[None, :, :, None]
    k_dec = jnp.exp((chunk - 1.0 - idx)[:, None] * log_g[None, :])[None, :, :, None]
    s_dec = jnp.exp(chunk * log_g)[None, :, None, None]

    def to_chunks(t):
        return jnp.moveaxis(t.astype(jnp.float32).reshape(b, n, chunk, nh, t.shape[-1]), 1, 0)

    def step(s, inp):
        qc, kc, vc = inp
        att = jnp.einsum('bihd,bjhd->bhij', qc, kc) * decay[None]
        o = jnp.einsum('bhij,bjhe->bihe', att, vc) + jnp.einsum('bihd,bhde->bihe', qc, s) * q_dec
        s = s * s_dec + jnp.einsum('bjhd,bjhe->bhde', kc * k_dec, vc)
        return s, o

    state, o = lax.scan(step, state, (to_chunks(q), to_chunks(k), to_chunks(v)))
    return jnp.moveaxis(o, 0, 1).reshape(b, l, nh, dv), state


def _retention(h, pos, state, wq, wk, wv, wg, wo, chunk):
    b, l, _ = h.shape
    q = _rope((h @ wq).reshape(b, l, RET_HEADS, RET_DK), pos, RET_DK, RET_THETA)
    k = _rope((h @ wk).reshape(b, l, RET_HEADS, RET_DK), pos, RET_DK, RET_THETA) * (RET_DK ** -0.5)
    v = (h @ wv).reshape(b, l, RET_HEADS, RET_DV)
    o, new_state = _retention_scan(q, k, v, state.astype(jnp.float32), chunk)
    o = _rmsnorm(o.astype(h.dtype)).reshape(b, l, RET_HEADS * RET_DV)
    return (o * jax.nn.silu(h @ wg)) @ wo, new_state


def _conv_ffn(h, buf, w_up, w_conv, b_conv, w_down):
    l = h.shape[1]
    a, g = jnp.split(h @ w_up, 2, axis=-1)
    a_ext = jnp.concatenate([buf.astype(a.dtype), a], axis=1)
    conv = b_conv
    for i in range(CONV_W):
        conv = conv + a_ext[:, i:i + l] * w_conv[i]
    return (jax.nn.silu(conv) * g) @ w_down, a_ext[:, l:]


def _shared_kv(x, c, gain, w_mod, b_mod, wk, wv, pos):
    b, l, _ = x.shape
    shift, scale = _modulation(c, w_mod, b_mod, 2)
    h = _ada_norm(x, gain, shift, scale)
    k = _rope((h @ wk).reshape(b, l, DIFF_HEADS, 2, DIFF_DH), pos, ROPE_DIM, ROPE_THETA)
    v = (h @ wv).reshape(b, l, DIFF_HEADS, DIFF_DV)
    return k, v


def _diff_lambda(lq1, lk1, lq2, lk2, lam_init):
    f32 = jnp.float32
    return (jnp.exp(jnp.sum(lq1.astype(f32) * lk1.astype(f32)))
            - jnp.exp(jnp.sum(lq2.astype(f32) * lk2.astype(f32))) + lam_init)


def _diff_query(h, wq, pos):
    b, l, _ = h.shape
    q = _rope((h @ wq).reshape(b, l, DIFF_HEADS, 2, DIFF_DH), pos, ROPE_DIM, ROPE_THETA)
    return q * (DIFF_DH ** -0.5)


def _diff_core(q, k, v, mask, lam):
    s = jnp.einsum('bqhcd,bkhcd->bhcqk', q, k).astype(jnp.float32)
    p = jax.nn.softmax(jnp.where(mask, s, -jnp.inf), axis=-1)
    a = p[:, :, 0] - lam * p[:, :, 1]
    return jnp.einsum('bhqk,bkhe->bqhe', a.astype(v.dtype), v)


def _diff_attend_prompt(q, k, v, lam):
    b, s = q.shape[0], q.shape[1]
    nb = s // Q_BLOCK
    qb = jnp.moveaxis(q.reshape((b, nb, Q_BLOCK) + q.shape[2:]), 1, 0)
    kpos = jnp.arange(s)

    def blk(args):
        qi, i = args
        qpos = i * Q_BLOCK + jnp.arange(Q_BLOCK)
        return _diff_core(qi, k, v, kpos[None, :] <= qpos[:, None], lam)

    o = lax.map(blk, (qb, jnp.arange(nb)))
    return jnp.moveaxis(o, 0, 1).reshape(b, s, DIFF_HEADS, DIFF_DV)


def _diff_attend_sample(q, k_new, v_new, cache_k, cache_v, page_table, lam):
    l = q.shape[1]
    past = page_table.shape[1] * cache_k.shape[1]
    kidx = jnp.arange(past + l)
    mask = (kidx[None, :] < past) | (kidx[None, :] - past <= jnp.arange(l)[:, None])

    def one(args):
        qi, kn, vn, pt = args
        kp = cache_k[pt].reshape((past,) + cache_k.shape[2:])
        vp = cache_v[pt].reshape((past,) + cache_v.shape[2:])
        k_all = jnp.concatenate([kp, kn.astype(kp.dtype)], axis=0)[None]
        v_all = jnp.concatenate([vp, vn.astype(vp.dtype)], axis=0)[None]
        return _diff_core(qi[None], k_all, v_all, mask, lam)[0]

    return lax.map(one, (q, k_new, v_new, page_table))


def _diff_out(o, subln, lam_init, wo):
    b, l = o.shape[0], o.shape[1]
    o = _rmsnorm(o, subln) * (1.0 - lam_init)
    return o.reshape(b, l, DIFF_HEADS * DIFF_DV) @ wo


def setup_inputs(seed: int = 0) -> dict:
    key = jax.random.key(seed)
    keys = iter(jax.random.split(key, 48))
    D = D_MODEL

    def nrm(shape, scale):
        return jax.random.normal(next(keys), shape, jnp.float32) * scale

    n_pages = PAST_LEN // PAGE_SIZE
    n_pool = (5 * DEC_BATCH * n_pages + 3) // 4
    page_table = jax.random.permutation(next(keys), n_pool)[:DEC_BATCH * n_pages]
    page_table = page_table.reshape(DEC_BATCH, n_pages).astype(jnp.int32)
    return {
        'x_prompt': nrm((BATCH, SEQ, D), 1.0),
        'x_sample': nrm((DEC_BATCH, DEC_SEQ, D), 1.0),
        'state_ret': nrm((N_A, DEC_BATCH, RET_HEADS, RET_DK, RET_DV), 0.5),
        'state_conv': nrm((DEPTH, DEC_BATCH, CONV_W - 1, D_FF), 1.0),
        'cache_k': nrm((n_pool, PAGE_SIZE, DIFF_HEADS, 2, DIFF_DH), 1.0),
        'cache_v': nrm((n_pool, PAGE_SIZE, DIFF_HEADS, DIFF_DV), 1.0),
        'page_table': page_table,
        'c_prompt': nrm((BATCH, D), 1.0),
        'c_sample': nrm((DEC_BATCH, D), 1.0),
        'ret_wq': nrm((N_A, D, RET_HEADS * RET_DK), D ** -0.5),
        'ret_wk': nrm((N_A, D, RET_HEADS * RET_DK), D ** -0.5),
        'ret_wv': nrm((N_A, D, RET_HEADS * RET_DV), D ** -0.5),
        'ret_wg': nrm((N_A, D, RET_HEADS * RET_DV), D ** -0.5),
        'ret_wo': nrm((N_A, RET_HEADS * RET_DV, D), (RET_HEADS * RET_DV) ** -0.5),
        'kv_norm': 1.0 + nrm((D,), 0.02),
        'kv_wmod': nrm((D, 2 * D), 0.5 * D ** -0.5),
        'kv_bmod': nrm((2 * D,), 0.02),
        'kv_wk': nrm((D, DIFF_HEADS * 2 * DIFF_DH), D ** -0.5),
        'kv_wv': nrm((D, DIFF_HEADS * DIFF_DV), D ** -0.5),
        'diff_wq': nrm((N_B, D, DIFF_HEADS * 2 * DIFF_DH), D ** -0.5),
        'diff_lq1': nrm((N_B, DIFF_DH), 0.1),
        'diff_lk1': nrm((N_B, DIFF_DH), 0.1),
        'diff_lq2': nrm((N_B, DIFF_DH), 0.1),
        'diff_lk2': nrm((N_B, DIFF_DH), 0.1),
        'diff_subln': 1.0 + nrm((N_B, DIFF_DV), 0.02),
        'diff_wo': nrm((N_B, DIFF_HEADS * DIFF_DV, D), (DIFF_HEADS * DIFF_DV) ** -0.5),
        'norm_mix': 1.0 + nrm((DEPTH, D), 0.02),
        'norm_ffn': 1.0 + nrm((DEPTH, D), 0.02),
        'w_mod': nrm((DEPTH, D, 6 * D), 0.5 * D ** -0.5),
        'b_mod': nrm((DEPTH, 6 * D), 0.02),
        'ffn_wup': nrm((DEPTH, D, 2 * D_FF), D ** -0.5),
        'ffn_conv': nrm((DEPTH, CONV_W, D_FF), CONV_W ** -0.5),
        'ffn_conv_b': nrm((DEPTH, D_FF), 0.02),
        'ffn_wdown': nrm((DEPTH, D_FF, D), D_FF ** -0.5),
        'norm_final': 1.0 + nrm((D,), 0.02),
    }


def reference(x_prompt, x_sample, state_ret, state_conv, cache_k, cache_v, page_table, c_prompt, c_sample,
              ret_wq, ret_wk, ret_wv, ret_wg, ret_wo,
              kv_norm, kv_wmod, kv_bmod, kv_wk, kv_wv,
              diff_wq, diff_lq1, diff_lk1, diff_lq2, diff_lk2, diff_subln, diff_wo,
              norm_mix, norm_ffn, w_mod, b_mod, ffn_wup, ffn_conv, ffn_conv_b, ffn_wdown, norm_final):
    bp, lp, _ = x_prompt.shape
    bs, ls, _ = x_sample.shape
    past_len = page_table.shape[1] * cache_k.shape[1]
    pos_p = jnp.arange(lp)
    pos_s = past_len + jnp.arange(ls)
    xp, xs = x_prompt, x_sample
    ret_p, ret_s, conv_p, conv_s = [], [], [], []
    k_p = v_p = k_s = v_s = None
    for l in range(DEPTH):
        sm_p, cm_p, gm_p, sf_p, cf_p, gf_p = _modulation(c_prompt, w_mod[l], b_mod[l], 6)
        sm_s, cm_s, gm_s, sf_s, cf_s, gf_s = _modulation(c_sample, w_mod[l], b_mod[l], 6)
        hp = _ada_norm(xp, norm_mix[l], sm_p, cm_p)
        hs = _ada_norm(xs, norm_mix[l], sm_s, cm_s)
        if l < N_A:
            init = jnp.zeros((bp, RET_HEADS, RET_DK, RET_DV), jnp.float32)
            op, st_p = _retention(hp, pos_p, init, ret_wq[l], ret_wk[l], ret_wv[l], ret_wg[l], ret_wo[l], RET_CHUNK)
            os_, st_s = _retention(hs, pos_s, state_ret[l], ret_wq[l], ret_wk[l], ret_wv[l], ret_wg[l], ret_wo[l], ls)
            ret_p.append(st_p)
            ret_s.append(st_s)
        else:
            bi = l - N_A
            if bi == 0:
                k_p, v_p = _shared_kv(xp, c_prompt, kv_norm, kv_wmod, kv_bmod, kv_wk, kv_wv, pos_p)
                k_s, v_s = _shared_kv(xs, c_sample, kv_norm, kv_wmod, kv_bmod, kv_wk, kv_wv, pos_s)
            lam_init = 0.8 - 0.6 * math.exp(-0.3 * l)
            lam = _diff_lambda(diff_lq1[bi], diff_lk1[bi], diff_lq2[bi], diff_lk2[bi], lam_init)
            qp = _diff_query(hp, diff_wq[bi], pos_p)
            qs = _diff_query(hs, diff_wq[bi], pos_s)
            op = _diff_out(_diff_attend_prompt(qp, k_p, v_p, lam), diff_subln[bi], lam_init, diff_wo[bi])
            os_ = _diff_out(_diff_attend_sample(qs, k_s, v_s, cache_k, cache_v, page_table, lam),
                            diff_subln[bi], lam_init, diff_wo[bi])
        xp = xp + gm_p[:, None, :] * op
        xs = xs + gm_s[:, None, :] * os_
        zero_buf = jnp.zeros((bp, CONV_W - 1, D_FF), xp.dtype)
        fp, bufp = _conv_ffn(_ada_norm(xp, norm_ffn[l], sf_p, cf_p), zero_buf,
                             ffn_wup[l], ffn_conv[l], ffn_conv_b[l], ffn_wdown[l])
        fs, bufs = _conv_ffn(_ada_norm(xs, norm_ffn[l], sf_s, cf_s), state_conv[l],
                             ffn_wup[l], ffn_conv[l], ffn_conv_b[l], ffn_wdown[l])
        xp = xp + gf_p[:, None, :] * fp
        xs = xs + gf_s[:, None, :] * fs
        conv_p.append(bufp)
        conv_s.append(bufs)
    y_prompt = _rmsnorm(xp, norm_final)
    y_sample = _rmsnorm(xs, norm_final)
    new_ret_prompt = jnp.stack(ret_p)
    new_ret_sample = jnp.stack(ret_s)
    new_conv_prompt = jnp.stack(conv_p)
    new_conv_sample = jnp.stack(conv_s)
    return (y_prompt, y_sample, new_ret_prompt, new_ret_sample, new_conv_prompt, new_conv_sample,
            k_p, v_p, k_s, v_s)
```

```python
import functools
import math

import jax
import jax.numpy as jnp
from jax import lax
from jax.experimental import pallas as pl
from jax.experimental.pallas import tpu as pltpu

F32 = jnp.float32
BF16 = jnp.bfloat16

NORM_EPS = 1e-6
RET_THETA = 10000.0
ROPE_THETA = 500000.0
RET_CHUNK = 128
CONV_W = 3

V7X_VMEM_LIMIT_BYTES = 56 * 1024 * 1024
LANES = 128
SUBLANES = 8


def _params(semantics):
    return pltpu.CompilerParams(dimension_semantics=semantics,
                                vmem_limit_bytes=V7X_VMEM_LIMIT_BYTES)


def _silu(x):
    return x * jax.nn.sigmoid(x)


def _rms(x):
    return x * lax.rsqrt(jnp.mean(x * x, axis=-1, keepdims=True) + NORM_EPS)


def _dot_nt(a, b):
    return lax.dot_general(a, b, (((1,), (1,)), ((), ())), preferred_element_type=F32)


def _dot_tn(a, b):
    return lax.dot_general(a, b, (((0,), (0,)), ((), ())), preferred_element_type=F32)


def _mod_kernel(c_ref, w_ref, b_ref, o_ref, s_ref):
    @pl.when(pl.program_id(0) == 0)
    def _():
        s_ref[...] = _silu(c_ref[...]).astype(BF16)

    acc = jnp.dot(s_ref[...], w_ref[...].astype(BF16), preferred_element_type=F32)
    o_ref[...] = acc + b_ref[...]


def _modulation(c, w, b, layer, tn=512):
    m, d = c.shape
    n = w.shape[2]
    return pl.pallas_call(
        _mod_kernel,
        out_shape=jax.ShapeDtypeStruct((m, n), F32),
        grid=(n // tn,),
        in_specs=[pl.BlockSpec((m, d), lambda j: (0, 0)),
                  pl.BlockSpec((None, d, tn), lambda j: (layer, 0, j)),
                  pl.BlockSpec((None, 1, tn), lambda j: (layer, 0, j))],
        out_specs=pl.BlockSpec((m, tn), lambda j: (0, j)),
        scratch_shapes=[pltpu.VMEM((m, d), BF16)],
        compiler_params=_params(("arbitrary",)),
        name="modulation",
    )(c, w, b.reshape(b.shape[0], 1, n))


NORM_ROWS = 128


def _ada_norm_to(h_ref, x_ref, gain_ref, shift_ref, scale_ref):
    tm = x_ref.shape[0]
    if shift_ref.shape[1] != 1:
        y = _rms(x_ref[...].astype(F32)) * gain_ref[...]
        h_ref[...] = (y * (1.0 + scale_ref[0]) + shift_ref[0]).astype(BF16)
        return

    def body(r, carry):
        rows = pl.ds(pl.multiple_of(r * NORM_ROWS, NORM_ROWS), NORM_ROWS)
        y = _rms(x_ref[rows, :].astype(F32)) * gain_ref[...]
        h_ref[rows, :] = (y * (1.0 + scale_ref[0]) + shift_ref[0]).astype(BF16)
        return carry

    lax.fori_loop(0, tm // NORM_ROWS, body, 0)


def _linear_kernel(*refs, has_norm, epi, epi_scale):
    it = iter(refs)
    x_ref = next(it)
    if has_norm:
        gain_ref, shift_ref, scale_ref = next(it), next(it), next(it)
    w_ref = next(it)
    if epi == "rope_ret":
        cos_ref, sin_ref = next(it), next(it)
    elif epi == "rope_diff":
        c_ref, sa_ref, sb_ref = next(it), next(it), next(it)
    elif epi == "residual":
        res_ref, gate_ref = next(it), next(it)
    o_ref = next(it)

    if has_norm:
        h_ref = next(it)

        @pl.when(pl.program_id(1) == 0)
        def _():
            _ada_norm_to(h_ref, x_ref, gain_ref, shift_ref, scale_ref)

        a = h_ref[...]
    else:
        a = x_ref[...].astype(BF16)

    acc = jnp.dot(a, w_ref[...], preferred_element_type=F32)
    tn = acc.shape[1]

    if epi is None:
        o_ref[...] = acc.astype(o_ref.dtype)
    elif epi == "rope_ret":
        cos, sin = cos_ref[...], sin_ref[...]
        for c in range(tn // (2 * LANES)):
            lo, mid, hi = 2 * c * LANES, (2 * c + 1) * LANES, (2 * c + 2) * LANES
            x1, x2 = acc[:, lo:mid], acc[:, mid:hi]
            o_ref[:, lo:mid] = ((x1 * cos - x2 * sin) * epi_scale).astype(o_ref.dtype)
            o_ref[:, mid:hi] = ((x2 * cos + x1 * sin) * epi_scale).astype(o_ref.dtype)
    elif epi == "rope_diff":
        cc, sa, sb = c_ref[...], sa_ref[...], sb_ref[...]
        for c in range(tn // LANES):
            xc = acc[:, c * LANES:(c + 1) * LANES]
            r = (xc * cc + pltpu.roll(xc, 16, 1) * sa
                 + pltpu.roll(xc, LANES - 16, 1) * sb)
            o_ref[:, c * LANES:(c + 1) * LANES] = (r * epi_scale).astype(o_ref.dtype)
    elif epi == "residual":
        o_ref[...] = (res_ref[...] + gate_ref[0] * acc).astype(o_ref.dtype)


def _group_spec(arr, tm, rows_per_group, width, col_fn):
    g, r, _ = arr.shape
    if r == 1:
        return pl.BlockSpec((1, 1, width),
                            lambda i, j: ((i * tm) // rows_per_group, 0, col_fn(j)))
    return pl.BlockSpec((1, r, width), lambda i, j: (0, 0, col_fn(j)))


def _table_spec(tab, tm, rows_per_group):
    if tab.shape[0] == 1:
        return pl.BlockSpec((1, LANES), lambda i, j: (0, 0))
    nblk = rows_per_group // tm
    return pl.BlockSpec((tm, LANES), lambda i, j: (i % nblk, 0))


def _linear(x, w, *, tm, tn, out_dtype, rows_per_group, norm=None, epi=None,
            epi_args=(), epi_scale=1.0, name="linear"):
    m, k = x.shape
    n = w.shape[1]
    has_norm = norm is not None
    args = [x]
    specs = [pl.BlockSpec((tm, k), lambda i, j: (i, 0))]
    if has_norm:
        gain, shift, scale = norm
        args += [gain.reshape(1, k), shift, scale]
        specs += [pl.BlockSpec((1, k), lambda i, j: (0, 0)),
                  _group_spec(shift, tm, rows_per_group, k, lambda j: 0),
                  _group_spec(scale, tm, rows_per_group, k, lambda j: 0)]
    args.append(w)
    specs.append(pl.BlockSpec((k, tn), lambda i, j: (0, j)))
    if epi in ("rope_ret", "rope_diff"):
        for tab in epi_args:
            args.append(tab)
            specs.append(_table_spec(tab, tm, rows_per_group))
    elif epi == "residual":
        res, gate = epi_args
        args += [res, gate]
        specs += [pl.BlockSpec((tm, tn), lambda i, j: (i, j)),
                  _group_spec(gate, tm, rows_per_group, tn, lambda j: j)]
    kern = functools.partial(_linear_kernel, has_norm=has_norm, epi=epi,
                             epi_scale=epi_scale)
    return pl.pallas_call(
        kern,
        out_shape=jax.ShapeDtypeStruct((m, n), out_dtype),
        grid=(m // tm, pl.cdiv(n, tn)),
        in_specs=specs,
        out_specs=pl.BlockSpec((tm, tn), lambda i, j: (i, j)),
        scratch_shapes=[pltpu.VMEM((tm, k), BF16)] if has_norm else [],
        compiler_params=_params(("parallel", "arbitrary")),
        name=name,
    )(*args)


def _ffn_up_kernel(*refs, decode):
    if decode:
        (x_ref, gain_ref, shift_ref, scale_ref, wa_ref, wg_ref, buf_ref, cw_ref,
         cb_ref, a_ref, u_ref, h_ref) = refs
    else:
        (x_ref, gain_ref, shift_ref, scale_ref, wa_ref, wg_ref,
         a_ref, g_ref, tail_ref, h_ref) = refs

    @pl.when(pl.program_id(1) == 0)
    def _():
        _ada_norm_to(h_ref, x_ref, gain_ref, shift_ref, scale_ref)

    h = h_ref[...]
    a = jnp.dot(h, wa_ref[...], preferred_element_type=F32)
    g = jnp.dot(h, wg_ref[...], preferred_element_type=F32)
    if decode:
        conv = (cb_ref[...] + buf_ref[0] * cw_ref[0:1, :] + buf_ref[1] * cw_ref[1:2, :]
                + a * cw_ref[2:3, :])
        a_ref[...] = a
        u_ref[...] = (_silu(conv) * g).astype(BF16)
    else:
        a_ref[...] = a.astype(BF16)
        g_ref[...] = g.astype(BF16)
        tail_ref[0] = a[a.shape[0] - SUBLANES:, :]


def _ffn_up(x, norm, wa, wg, *, tm, tn, rows_per_group, decode_args=None, name="ffn_up"):
    m, k = x.shape
    f = wa.shape[1]
    gain, shift, scale = norm
    decode = decode_args is not None
    args = [x, gain.reshape(1, k), shift, scale, wa, wg]
    specs = [pl.BlockSpec((tm, k), lambda i, j: (i, 0)),
             pl.BlockSpec((1, k), lambda i, j: (0, 0)),
             _group_spec(shift, tm, rows_per_group, k, lambda j: 0),
             _group_spec(scale, tm, rows_per_group, k, lambda j: 0),
             pl.BlockSpec((k, tn), lambda i, j: (0, j)),
             pl.BlockSpec((k, tn), lambda i, j: (0, j))]
    tile = pl.BlockSpec((tm, tn), lambda i, j: (i, j))
    if decode:
        buf, cw, cb = decode_args
        args += [buf, cw, cb]
        specs += [pl.BlockSpec((CONV_W - 1, tm, tn), lambda i, j: (0, i, j)),
                  pl.BlockSpec((CONV_W, tn), lambda i, j: (0, j)),
                  pl.BlockSpec((1, tn), lambda i, j: (0, j))]
        out_shape = [jax.ShapeDtypeStruct((m, f), F32), jax.ShapeDtypeStruct((m, f), BF16)]
        out_specs = [tile, tile]
    else:
        out_shape = [jax.ShapeDtypeStruct((m, f), BF16), jax.ShapeDtypeStruct((m, f), BF16),
                     jax.ShapeDtypeStruct((m // tm, SUBLANES, f), F32)]
        out_specs = [tile, tile, pl.BlockSpec((1, SUBLANES, tn), lambda i, j: (i, 0, j))]
    return pl.pallas_call(
        functools.partial(_ffn_up_kernel, decode=decode),
        out_shape=out_shape,
        grid=(m // tm, pl.cdiv(f, tn)),
        in_specs=specs,
        out_specs=out_specs,
        scratch_shapes=[pltpu.VMEM((tm, k), BF16)],
        compiler_params=_params(("parallel", "arbitrary")),
        name=name,
    )(*args)


HALO_ROWS = 16


def _conv_gate_kernel(a_ref, halo_ref, g_ref, cw_ref, cb_ref, u_ref):
    tl, f = a_ref.shape
    first = pl.program_id(1) == 0
    row = lax.broadcasted_iota(jnp.int32, (tl, LANES), 0)
    for c in range(f // LANES):
        cols = slice(c * LANES, (c + 1) * LANES)
        a = a_ref[:, cols].astype(F32)
        halo = jnp.where(first, 0.0, halo_ref[:, cols].astype(F32))
        prev1 = halo[HALO_ROWS - 1:HALO_ROWS, :]
        prev2 = halo[HALO_ROWS - 2:HALO_ROWS - 1, :]
        a_m1 = jnp.where(row == 0, prev1, pltpu.roll(a, 1, 0))
        a_m2 = jnp.where(row == 0, prev2, jnp.where(row == 1, prev1, pltpu.roll(a, 2, 0)))
        conv = (cb_ref[:, cols] + a_m2 * cw_ref[0:1, cols] + a_m1 * cw_ref[1:2, cols]
                + a * cw_ref[2:3, cols])
        u_ref[:, cols] = (_silu(conv) * g_ref[:, cols].astype(F32)).astype(BF16)


def _conv_gate(a, g, cw, cb, *, batch, seq, tl=256):
    m, f = a.shape
    nl = seq // tl
    hb = tl // HALO_ROWS
    return pl.pallas_call(
        _conv_gate_kernel,
        out_shape=jax.ShapeDtypeStruct((m, f), BF16),
        grid=(batch, nl),
        in_specs=[pl.BlockSpec((tl, f), lambda b, l: (b * nl + l, 0)),
                  pl.BlockSpec((HALO_ROWS, f),
                               lambda b, l: (jnp.maximum((b * nl + l) * hb - 1, 0), 0)),
                  pl.BlockSpec((tl, f), lambda b, l: (b * nl + l, 0)),
                  pl.BlockSpec((CONV_W, f), lambda b, l: (0, 0)),
                  pl.BlockSpec((1, f), lambda b, l: (0, 0))],
        out_specs=pl.BlockSpec((tl, f), lambda b, l: (b * nl + l, 0)),
        compiler_params=_params(("parallel", "parallel")),
        name="conv_gate",
    )(a, a, g, cw, cb)


def _ret_prompt_kernel(sdec_ref, q_ref, k_ref, v_ref, g_ref, decay_ref, qdec_ref,
                       kdec_ref, o_ref, st_ref):
    h = pl.program_id(1)

    @pl.when(pl.program_id(2) == 0)
    def _():
        st_ref[...] = jnp.zeros_like(st_ref)

    s = st_ref[0, 0, 0]
    q, k, v = q_ref[...], k_ref[...], v_ref[...]
    att = _dot_nt(q, k) * decay_ref[0]
    o = (jnp.dot(att.astype(BF16), v, preferred_element_type=F32)
         + jnp.dot(q, s.astype(BF16), preferred_element_type=F32) * qdec_ref[0])
    kd = (k.astype(F32) * kdec_ref[0]).astype(BF16)
    st_ref[0, 0, 0] = s * sdec_ref[h] + _dot_tn(kd, v)
    o_ref[...] = (_rms(o) * _silu(g_ref[...].astype(F32))).astype(BF16)


def _ret_tables(chunk, nh):
    log_g = jnp.log1p(-jnp.exp2(-5.0 - jnp.arange(nh, dtype=F32)))
    idx = jnp.arange(chunk, dtype=F32)
    rel = idx[:, None] - idx[None, :]
    decay = jnp.where(rel >= 0, jnp.exp(jnp.maximum(rel, 0.0)[None] * log_g[:, None, None]), 0.0)
    q_dec = jnp.exp((idx[None, :] + 1.0) * log_g[:, None])[:, :, None]
    k_dec = jnp.exp((chunk - 1.0 - idx)[None, :] * log_g[:, None])[:, :, None]
    s_dec = jnp.exp(chunk * log_g)
    return decay, q_dec, k_dec, s_dec


def _retention_prompt(q, k, v, g, *, batch, seq, nh, dk, dv, chunk):
    m = q.shape[0]
    nc = seq // chunk
    decay, q_dec, k_dec, s_dec = _ret_tables(chunk, nh)
    row = lambda b, h, n: (b * nc + n, h)
    head = lambda b, h, n: (h, 0, 0)
    return pl.pallas_call(
        _ret_prompt_kernel,
        out_shape=[jax.ShapeDtypeStruct((m, nh * dv), BF16),
                   jax.ShapeDtypeStruct((1, batch, nh, dk, dv), F32)],
        grid=(batch, nh, nc),
        in_specs=[pl.BlockSpec(memory_space=pltpu.SMEM),
                  pl.BlockSpec((chunk, dk), row),
                  pl.BlockSpec((chunk, dk), row),
                  pl.BlockSpec((chunk, dv), row),
                  pl.BlockSpec((chunk, dv), row),
                  pl.BlockSpec((1, chunk, chunk), head),
                  pl.BlockSpec((1, chunk, 1), head),
                  pl.BlockSpec((1, chunk, 1), head)],
        out_specs=[pl.BlockSpec((chunk, dv), row),
                   pl.BlockSpec((1, 1, 1, dk, dv), lambda b, h, n: (0, b, h, 0, 0))],
        compiler_params=_params(("parallel", "parallel", "arbitrary")),
        name="retention_prompt",
    )(s_dec, q, k, v, g, decay, q_dec, k_dec)


def _ret_decode_kernel(q_ref, k_ref, v_ref, g_ref, st_ref, dec_ref, o_ref, sto_ref, *, nh):
    q, k, v = q_ref[0], k_ref[0], v_ref[0]
    decay, qdec, kdec, sdec = (dec_ref[:, i:i + 1] for i in range(4))
    att = jnp.sum(q * k, axis=-1, keepdims=True) * decay
    rows = lax.broadcasted_iota(jnp.int32, (nh, 1), 0)
    qb, vb = q.astype(BF16), v.astype(BF16)
    kd = k * kdec
    qs = jnp.zeros(v.shape, F32)
    for h in range(nh):
        s = st_ref[0, 0, h]
        qs = jnp.where(rows == h, jnp.dot(qb, s.astype(BF16), preferred_element_type=F32), qs)
        k_h = jnp.where(rows == h, kd, 0.0).astype(BF16)
        sto_ref[0, 0, h] = s * sdec[h:h + 1, :] + _dot_tn(k_h, vb)
    o = att * v + qs * qdec
    o_ref[0] = _rms(o) * _silu(g_ref[0])


def _retention_decode(q, k, v, g, state, *, nh, dk, dv):
    b = q.shape[0]
    decay, q_dec, k_dec, s_dec = _ret_tables(1, nh)
    dec = jnp.stack([decay[:, 0, 0], q_dec[:, 0, 0], k_dec[:, 0, 0], s_dec], axis=1)
    tok = lambda w: pl.BlockSpec((1, nh, w), lambda i: (i, 0, 0))
    st = pl.BlockSpec((1, 1, nh, dk, dv), lambda i: (0, i, 0, 0, 0))
    return pl.pallas_call(
        functools.partial(_ret_decode_kernel, nh=nh),
        out_shape=[jax.ShapeDtypeStruct((b, nh, dv), F32),
                   jax.ShapeDtypeStruct(state.shape, F32)],
        grid=(b,),
        in_specs=[tok(dk), tok(dk), tok(dv), tok(dv), st,
                  pl.BlockSpec((nh, 4), lambda i: (0, 0))],
        out_specs=[tok(dv), st],
        compiler_params=_params(("parallel",)),
        name="retention_decode",
    )(q.reshape(b, nh, dk), k.reshape(b, nh, dk), v.reshape(b, nh, dv),
      g.reshape(b, nh, dv), state, dec)


def _online_softmax_step(s, v, m_ref, l_ref, acc_ref):
    m_old = m_ref[...]
    m_new = jnp.maximum(m_old, jnp.max(s, axis=-1, keepdims=True))
    alpha = jnp.exp(m_old - m_new)
    p = jnp.exp(s - m_new)
    l_ref[...] = alpha * l_ref[...] + jnp.sum(p, axis=-1, keepdims=True)
    acc_ref[...] = alpha * acc_ref[...] + jnp.dot(p.astype(BF16), v,
                                                  preferred_element_type=F32)
    m_ref[...] = m_new


def _diff_out(o, subln, out_scale):
    return _rms(o) * subln * out_scale


def _diff_prompt_kernel(lam_ref, q_ref, k_ref, v_ref, subln_ref, o_ref,
                        kb_ref, vb_ref, m_ref, l_ref, acc_ref, *, tq, dh):
    qi = pl.program_id(2)

    @pl.when(qi == 0)
    def _():
        kb_ref[...] = k_ref[...].astype(BF16)
        vb_ref[...] = v_ref[...].astype(BF16)

    m_ref[...] = jnp.full_like(m_ref, -jnp.inf)
    l_ref[...] = jnp.zeros_like(l_ref)
    acc_ref[...] = jnp.zeros_like(acc_ref)
    q = q_ref[...]

    def block(start, causal):
        kblk = kb_ref[pl.ds(start, tq), :]
        vblk = vb_ref[pl.ds(start, tq), :]
        for c in range(2):
            s = _dot_nt(q[:, c * dh:(c + 1) * dh], kblk[:, c * dh:(c + 1) * dh])
            if causal:
                row = lax.broadcasted_iota(jnp.int32, s.shape, 0)
                col = lax.broadcasted_iota(jnp.int32, s.shape, 1)
                s = jnp.where(col <= row, s, -jnp.inf)
            _online_softmax_step(s, vblk, m_ref.at[c], l_ref.at[c], acc_ref.at[c])

    def body(ki, carry):
        block(pl.multiple_of(ki * tq, tq), False)
        return carry

    lax.fori_loop(0, qi, body, 0)
    block(pl.multiple_of(qi * tq, tq), True)

    o = acc_ref[0] / l_ref[0] - lam_ref[0] * (acc_ref[1] / l_ref[1])
    o_ref[...] = _diff_out(o, subln_ref[...], lam_ref[1]).astype(BF16)


def _diff_attention_prompt(q, k, v, lam, subln, *, batch, seq, nh, dh, dv, tq=256):
    m = q.shape[0]
    nq = seq // tq
    return pl.pallas_call(
        functools.partial(_diff_prompt_kernel, tq=tq, dh=dh),
        out_shape=jax.ShapeDtypeStruct((m, nh * dv), BF16),
        grid=(batch, nh, nq),
        in_specs=[pl.BlockSpec(memory_space=pltpu.SMEM),
                  pl.BlockSpec((tq, 2 * dh), lambda b, h, i: (b * nq + i, h)),
                  pl.BlockSpec((seq, 2 * dh), lambda b, h, i: (b, h)),
                  pl.BlockSpec((seq, dv), lambda b, h, i: (b, h)),
                  pl.BlockSpec((1, dv), lambda b, h, i: (0, 0))],
        out_specs=pl.BlockSpec((tq, dv), lambda b, h, i: (b * nq + i, h)),
        scratch_shapes=[pltpu.VMEM((seq, 2 * dh), BF16), pltpu.VMEM((seq, dv), BF16),
                        pltpu.VMEM((2, tq, 1), F32), pltpu.VMEM((2, tq, 1), F32),
                        pltpu.VMEM((2, tq, dv), F32)],
        compiler_params=_params(("parallel", "parallel", "arbitrary")),
        name="diff_attention_prompt",
    )(lam, q, k, v, subln.reshape(1, dv))


def _diff_decode_kernel(pt_ref, lam_ref, q_ref, kn_ref, vn_ref, subln_ref, ck_ref, cv_ref,
                        o_ref, m_ref, l_ref, acc_ref, *, nh, page):
    del pt_ref
    p = pl.program_id(1)
    rows_kv = page * nh

    @pl.when(p == 0)
    def _():
        m_ref[...] = jnp.full_like(m_ref, -jnp.inf)
        l_ref[...] = jnp.zeros_like(l_ref)
        acc_ref[...] = jnp.zeros_like(acc_ref)

    q = q_ref[0]
    qb = q.astype(BF16)
    k0 = ck_ref[0, pl.ds(0, rows_kv, stride=2), :].astype(BF16)
    k1 = ck_ref[0, pl.ds(1, rows_kv, stride=2), :].astype(BF16)
    s = jnp.concatenate([_dot_nt(qb[:nh], k0), _dot_nt(qb[nh:], k1)], axis=0)
    row = lax.broadcasted_iota(jnp.int32, s.shape, 0)
    col = lax.broadcasted_iota(jnp.int32, s.shape, 1)
    s = jnp.where((col % nh) == (row % nh), s, -jnp.inf)
    _online_softmax_step(s, cv_ref[0].astype(BF16), m_ref, l_ref, acc_ref)

    @pl.when(p == pl.num_programs(1) - 1)
    def _():
        s_new = jnp.sum(q * kn_ref[0], axis=-1, keepdims=True)
        m_old = m_ref[...]
        m_new = jnp.maximum(m_old, s_new)
        alpha = jnp.exp(m_old - m_new)
        pn = jnp.exp(s_new - m_new)
        l = alpha * l_ref[...] + pn
        vn = vn_ref[0]
        acc = alpha * acc_ref[...] + pn * jnp.concatenate([vn, vn], axis=0)
        on = acc / l
        o = on[:nh] - lam_ref[0] * on[nh:]
        o_ref[0] = _diff_out(o, subln_ref[...], lam_ref[1])


def _diff_attention_decode(q, k_new, v_new, cache_k, cache_v, page_table, lam, subln,
                           *, nh, dh, dv):
    b = q.shape[0]
    n_pool, page = cache_k.shape[0], cache_k.shape[1]
    n_pages = page_table.shape[1]
    ck = cache_k.reshape(n_pool, page * nh * 2, dh)
    cv = cache_v.reshape(n_pool, page * nh, dv)
    comp_major = lambda t: t.reshape(b, nh, 2, dh).transpose(0, 2, 1, 3).reshape(b, 2 * nh, dh)
    tok = lambda r, w: pl.BlockSpec((1, r, w), lambda i, p, pt: (i, 0, 0))
    grid_spec = pltpu.PrefetchScalarGridSpec(
        num_scalar_prefetch=1,
        grid=(b, n_pages),
        in_specs=[pl.BlockSpec(memory_space=pltpu.SMEM),
                  tok(2 * nh, dh), tok(2 * nh, dh), tok(nh, dv),
                  pl.BlockSpec((1, dv), lambda i, p, pt: (0, 0)),
                  pl.BlockSpec((1, page * nh * 2, dh),
                               lambda i, p, pt: (pt[i * n_pages + p], 0, 0)),
                  pl.BlockSpec((1, page * nh, dv),
                               lambda i, p, pt: (pt[i * n_pages + p], 0, 0))],
        out_specs=tok(nh, dv),
        scratch_shapes=[pltpu.VMEM((2 * nh, 1), F32), pltpu.VMEM((2 * nh, 1), F32),
                        pltpu.VMEM((2 * nh, dv), F32)],
    )
    return pl.pallas_call(
        functools.partial(_diff_decode_kernel, nh=nh, page=page),
        out_shape=jax.ShapeDtypeStruct((b, nh, dv), F32),
        grid_spec=grid_spec,
        compiler_params=_params(("parallel", "arbitrary")),
        name="diff_attention_decode",
    )(page_table.reshape(-1), lam, comp_major(q), comp_major(k_new),
      v_new.reshape(b, nh, dv), subln.reshape(1, dv), ck, cv)


def _final_norm_kernel(x_ref, gain_ref, o_ref):
    o_ref[...] = _rms(x_ref[...]) * gain_ref[...]


def _final_norm(x, gain, tm):
    m, d = x.shape
    return pl.pallas_call(
        _final_norm_kernel,
        out_shape=jax.ShapeDtypeStruct((m, d), F32),
        grid=(m // tm,),
        in_specs=[pl.BlockSpec((tm, d), lambda i: (i, 0)),
                  pl.BlockSpec((1, d), lambda i: (0, 0))],
        out_specs=pl.BlockSpec((tm, d), lambda i: (i, 0)),
        compiler_params=_params(("parallel",)),
        name="final_norm",
    )(x, gain.reshape(1, d))


def _ret_rope_tables(pos, dk):
    half = dk // 2
    inv = jnp.float32(RET_THETA) ** (-jnp.arange(half, dtype=F32) / half)
    ang = pos.astype(F32)[:, None] * inv[None, :]
    return jnp.cos(ang), jnp.sin(ang)


def _diff_rope_tables(pos, dh, rope_dim):
    half = rope_dim // 2
    inv = jnp.float32(ROPE_THETA) ** (-jnp.arange(half, dtype=F32) / half)
    ang = pos.astype(F32)[:, None] * inv[None, :]
    cos, sin = jnp.cos(ang), jnp.sin(ang)
    n = pos.shape[0]
    rest = dh - rope_dim
    coef = jnp.concatenate([cos, cos, jnp.ones((n, rest), F32)], axis=1)
    from_lo = jnp.concatenate([jnp.zeros((n, half), F32), sin, jnp.zeros((n, rest), F32)], axis=1)
    from_hi = jnp.concatenate([-sin, jnp.zeros((n, half + rest), F32)], axis=1)
    return coef, from_lo, from_hi


def kernel(x_prompt, x_sample, state_ret, state_conv, cache_k, cache_v, page_table, c_prompt, c_sample, ret_wq, ret_wk, ret_wv, ret_wg, ret_wo, kv_norm, kv_wmod, kv_bmod, kv_wk, kv_wv, diff_wq, diff_lq1, diff_lk1, diff_lq2, diff_lk2, diff_subln, diff_wo, norm_mix, norm_ffn, w_mod, b_mod, ffn_wup, ffn_conv, ffn_conv_b, ffn_wdown, norm_final):
    bp, lp, d = x_prompt.shape
    bs, ls, _ = x_sample.shape
    assert ls == 1, "decode group carries one new token per sequence"
    depth = w_mod.shape[0]
    n_a = ret_wq.shape[0]
    nh_r, dk_r, dv_r = state_ret.shape[2], state_ret.shape[3], state_ret.shape[4]
    nh_d, dh_d, dv_d = cache_k.shape[2], cache_k.shape[4], cache_v.shape[3]
    rope_dim = dh_d // 4
    d_ff = ffn_conv.shape[2]
    past_len = page_table.shape[1] * cache_k.shape[1]
    mp = bp * lp
    tm = 1024

    pos_p = jnp.arange(lp)
    pos_s = past_len + jnp.arange(ls)
    bf = lambda w: w.astype(BF16)

    xp = x_prompt.reshape(mp, d)
    xs = x_sample.reshape(bs, d)

    n_c = bp + bs
    n_c_pad = -(-n_c // SUBLANES) * SUBLANES
    c_all = jnp.concatenate([c_prompt, c_sample, jnp.zeros((n_c_pad - n_c, d), F32)], axis=0)

    def split_mod(mod, n):
        mod_p = mod[:bp].reshape(bp, 1, n, d)
        mod_s = mod[bp:n_c].reshape(1, bs, n, d)
        return ([mod_p[:, :, i] for i in range(n)], [mod_s[:, :, i] for i in range(n)])

    lin_p = functools.partial(_linear, tm=tm, rows_per_group=lp)
    lin_s = functools.partial(_linear, tm=bs, rows_per_group=bs)

    ret_p, ret_s, conv_p, conv_s = [], [], [], []
    k_p = v_p = k_s = v_s = None
    for l in range(depth):
        (sm_p, cm_p, gm_p, sf_p, cf_p, gf_p), (sm_s, cm_s, gm_s, sf_s, cf_s, gf_s) = split_mod(
            _modulation(c_all, w_mod, b_mod, l), 6)
        nrm_p = (norm_mix[l], sm_p, cm_p)
        nrm_s = (norm_mix[l], sm_s, cm_s)
        if l < n_a:
            wq, wk, wv, wg, wo = bf(ret_wq[l]), bf(ret_wk[l]), bf(ret_wv[l]), bf(ret_wg[l]), bf(ret_wo[l])
            k_scale = dk_r ** -0.5
            tabs_p = _ret_rope_tables(pos_p, dk_r)
            tabs_s = _ret_rope_tables(pos_s, dk_r)
            q = lin_p(xp, wq, tn=512, out_dtype=BF16, norm=nrm_p, epi="rope_ret", epi_args=tabs_p, name="ret_q_p")
            k = lin_p(xp, wk, tn=512, out_dtype=BF16, norm=nrm_p, epi="rope_ret", epi_args=tabs_p,
                      epi_scale=k_scale, name="ret_k_p")
            v = lin_p(xp, wv, tn=512, out_dtype=BF16, norm=nrm_p, name="ret_v_p")
            g = lin_p(xp, wg, tn=512, out_dtype=BF16, norm=nrm_p, name="ret_g_p")
            o, st_p = _retention_prompt(q, k, v, g, batch=bp, seq=lp, nh=nh_r, dk=dk_r, dv=dv_r,
                                        chunk=RET_CHUNK)
            xp = lin_p(o, wo, tn=512, out_dtype=F32, epi="residual", epi_args=(xp, gm_p), name="ret_o_p")

            q = lin_s(xs, wq, tn=512, out_dtype=F32, norm=nrm_s, epi="rope_ret", epi_args=tabs_s, name="ret_q_s")
            k = lin_s(xs, wk, tn=512, out_dtype=F32, norm=nrm_s, epi="rope_ret", epi_args=tabs_s,
                      epi_scale=k_scale, name="ret_k_s")
            v = lin_s(xs, wv, tn=512, out_dtype=F32, norm=nrm_s, name="ret_v_s")
            g = lin_s(xs, wg, tn=512, out_dtype=F32, norm=nrm_s, name="ret_g_s")
            o, st_s = _retention_decode(q, k, v, g, state_ret[l:l + 1], nh=nh_r, dk=dk_r, dv=dv_r)
            xs = lin_s(o.reshape(bs, nh_r * dv_r), wo, tn=512, out_dtype=F32, epi="residual",
                       epi_args=(xs, gm_s), name="ret_o_s")
            ret_p.append(st_p[0])
            ret_s.append(st_s[0])
        else:
            bi = l - n_a
            tabs_p = _diff_rope_tables(pos_p, dh_d, rope_dim)
            tabs_s = _diff_rope_tables(pos_s, dh_d, rope_dim)
            if bi == 0:
                (sh_p, sc_p), (sh_s, sc_s) = split_mod(
                    _modulation(c_all, kv_wmod[None], kv_bmod[None], 0), 2)
                wk, wv = bf(kv_wk), bf(kv_wv)
                k_p = lin_p(xp, wk, tn=512, out_dtype=F32, norm=(kv_norm, sh_p, sc_p), epi="rope_diff",
                            epi_args=tabs_p, name="kv_k_p")
                v_p = lin_p(xp, wv, tn=512, out_dtype=F32, norm=(kv_norm, sh_p, sc_p), name="kv_v_p")
                k_s = lin_s(xs, wk, tn=512, out_dtype=F32, norm=(kv_norm, sh_s, sc_s), epi="rope_diff",
                            epi_args=tabs_s, name="kv_k_s")
                v_s = lin_s(xs, wv, tn=512, out_dtype=F32, norm=(kv_norm, sh_s, sc_s), name="kv_v_s")
            lam_init = 0.8 - 0.6 * math.exp(-0.3 * l)
            lam = (jnp.exp(jnp.sum(diff_lq1[bi].astype(F32) * diff_lk1[bi].astype(F32)))
                   - jnp.exp(jnp.sum(diff_lq2[bi].astype(F32) * diff_lk2[bi].astype(F32))) + lam_init)
            lam_sc = jnp.stack([lam, jnp.float32(1.0 - lam_init)])
            wq, wo = bf(diff_wq[bi]), bf(diff_wo[bi])
            q_scale = dh_d ** -0.5
            q = lin_p(xp, wq, tn=512, out_dtype=BF16, norm=nrm_p, epi="rope_diff", epi_args=tabs_p,
                      epi_scale=q_scale, name="diff_q_p")
            o = _diff_attention_prompt(q, k_p, v_p, lam_sc, diff_subln[bi], batch=bp, seq=lp, nh=nh_d,
                                       dh=dh_d, dv=dv_d)
            xp = lin_p(o, wo, tn=512, out_dtype=F32, epi="residual", epi_args=(xp, gm_p), name="diff_o_p")

            q = lin_s(xs, wq, tn=512, out_dtype=F32, norm=nrm_s, epi="rope_diff", epi_args=tabs_s,
                      epi_scale=q_scale, name="diff_q_s")
            o = _diff_attention_decode(q, k_s, v_s, cache_k, cache_v, page_table, lam_sc, diff_subln[bi],
                                       nh=nh_d, dh=dh_d, dv=dv_d)
            xs = lin_s(o.reshape(bs, nh_d * dv_d), wo, tn=512, out_dtype=F32, epi="residual",
                       epi_args=(xs, gm_s), name="diff_o_s")

        wa, wg_ = bf(ffn_wup[l][:, :d_ff]), bf(ffn_wup[l][:, d_ff:])
        wdn = bf(ffn_wdown[l])
        cw, cb = ffn_conv[l], ffn_conv_b[l].reshape(1, d_ff)
        a, g, tails = _ffn_up(xp, (norm_ffn[l], sf_p, cf_p), wa, wg_, tm=tm, tn=512, rows_per_group=lp,
                              name="ffn_up_p")
        u = _conv_gate(a, g, cw, cb, batch=bp, seq=lp)
        xp = lin_p(u, wdn, tn=512, out_dtype=F32, epi="residual", epi_args=(xp, gf_p), name="ffn_down_p")
        last = tails.reshape(bp, lp // tm, SUBLANES, d_ff)[:, -1, SUBLANES - (CONV_W - 1):, :]
        conv_p.append(last)

        buf = jnp.swapaxes(state_conv[l], 0, 1)
        a_s, u_s = _ffn_up(xs, (norm_ffn[l], sf_s, cf_s), wa, wg_, tm=bs, tn=512, rows_per_group=bs,
                           decode_args=(buf, cw, cb), name="ffn_up_s")
        xs = lin_s(u_s, wdn, tn=512, out_dtype=F32, epi="residual", epi_args=(xs, gf_s), name="ffn_down_s")
        conv_s.append(jnp.concatenate([state_conv[l][:, 1:], a_s[:, None, :]], axis=1))

    y_prompt = _final_norm(xp, norm_final, tm).reshape(bp, lp, d)
    y_sample = _final_norm(xs, norm_final, bs).reshape(bs, ls, d)
    return (y_prompt, y_sample, jnp.stack(ret_p), jnp.stack(ret_s), jnp.stack(conv_p), jnp.stack(conv_s),
            k_p.reshape(bp, lp, nh_d, 2, dh_d), v_p.reshape(bp, lp, nh_d, dv_d),
            k_s.reshape(bs, ls, nh_d, 2, dh_d), v_s.reshape(bs, ls, nh_d, dv_d))
```

```python
import functools
import math

import jax
import jax.numpy as jnp
from jax import lax
from jax.experimental import pallas as pl
from jax.experimental.pallas import tpu as pltpu

F32 = jnp.float32
BF16 = jnp.bfloat16

NORM_EPS = 1e-6
RET_THETA = 10000.0
ROPE_THETA = 500000.0
RET_CHUNK = 256
CONV_W = 3

V7X_VMEM_LIMIT_BYTES = 56 * 1024 * 1024
LANES = 128
SUBLANES = 8


def _params(semantics):
    return pltpu.CompilerParams(dimension_semantics=semantics,
                                vmem_limit_bytes=V7X_VMEM_LIMIT_BYTES)


def _silu(x):
    return x * jax.nn.sigmoid(x)


def _rms(x):
    return x * lax.rsqrt(jnp.mean(x * x, axis=-1, keepdims=True) + NORM_EPS)


def _dot_nt(a, b):
    return lax.dot_general(a, b, (((1,), (1,)), ((), ())), preferred_element_type=F32)


def _dot_tn(a, b):
    return lax.dot_general(a, b, (((0,), (0,)), ((), ())), preferred_element_type=F32)


def _mod_kernel(c_ref, w_ref, b_ref, o_ref, s_ref):
    @pl.when(pl.program_id(0) == 0)
    def _():
        s_ref[...] = _silu(c_ref[...]).astype(BF16)

    acc = jnp.dot(s_ref[...], w_ref[...].astype(BF16), preferred_element_type=F32)
    o_ref[...] = acc + b_ref[...]


def _modulation(c, w, b, layer, tn=512):
    m, d = c.shape
    n = w.shape[2]
    return pl.pallas_call(
        _mod_kernel,
        out_shape=jax.ShapeDtypeStruct((m, n), F32),
        grid=(n // tn,),
        in_specs=[pl.BlockSpec((m, d), lambda j: (0, 0)),
                  pl.BlockSpec((None, d, tn), lambda j: (layer, 0, j)),
                  pl.BlockSpec((None, 1, tn), lambda j: (layer, 0, j))],
        out_specs=pl.BlockSpec((m, tn), lambda j: (0, j)),
        scratch_shapes=[pltpu.VMEM((m, d), BF16)],
        compiler_params=_params(("arbitrary",)),
        name="modulation",
    )(c, w, b.reshape(b.shape[0], 1, n))


NORM_ROWS = 128


def _ada_norm_to(h_ref, x_ref, gain_ref, shift_ref, scale_ref):
    tm = x_ref.shape[0]
    if shift_ref.shape[1] != 1:
        y = _rms(x_ref[...].astype(F32)) * gain_ref[...]
        h_ref[...] = (y * (1.0 + scale_ref[0]) + shift_ref[0]).astype(BF16)
        return

    def body(r, carry):
        rows = pl.ds(pl.multiple_of(r * NORM_ROWS, NORM_ROWS), NORM_ROWS)
        y = _rms(x_ref[rows, :].astype(F32)) * gain_ref[...]
        h_ref[rows, :] = (y * (1.0 + scale_ref[0]) + shift_ref[0]).astype(BF16)
        return carry

    lax.fori_loop(0, tm // NORM_ROWS, body, 0)


def _linear_kernel(*refs, has_norm, epi, epi_scale):
    it = iter(refs)
    x_ref = next(it)
    if has_norm:
        gain_ref, shift_ref, scale_ref = next(it), next(it), next(it)
    w_ref = next(it)
    if epi == "rope_ret":
        cos_ref, sin_ref = next(it), next(it)
    elif epi == "rope_diff":
        c_ref, sa_ref, sb_ref = next(it), next(it), next(it)
    elif epi == "residual":
        res_ref, gate_ref = next(it), next(it)
    o_ref = next(it)

    if has_norm:
        h_ref = next(it)

        @pl.when(pl.program_id(1) == 0)
        def _():
            _ada_norm_to(h_ref, x_ref, gain_ref, shift_ref, scale_ref)

        a = h_ref[...]
    else:
        a = x_ref[...].astype(BF16)

    acc = jnp.dot(a, w_ref[...], preferred_element_type=F32)
    tn = acc.shape[1]

    if epi is None:
        o_ref[...] = acc.astype(o_ref.dtype)
    elif epi == "rope_ret":
        cos, sin = cos_ref[...], sin_ref[...]
        for c in range(tn // (2 * LANES)):
            lo, mid, hi = 2 * c * LANES, (2 * c + 1) * LANES, (2 * c + 2) * LANES
            x1, x2 = acc[:, lo:mid], acc[:, mid:hi]
            o_ref[:, lo:mid] = ((x1 * cos - x2 * sin) * epi_scale).astype(o_ref.dtype)
            o_ref[:, mid:hi] = ((x2 * cos + x1 * sin) * epi_scale).astype(o_ref.dtype)
    elif epi == "rope_diff":
        cc, sa, sb = c_ref[...], sa_ref[...], sb_ref[...]
        for c in range(tn // LANES):
            xc = acc[:, c * LANES:(c + 1) * LANES]
            r = (xc * cc + pltpu.roll(xc, 16, 1) * sa
                 + pltpu.roll(xc, LANES - 16, 1) * sb)
            o_ref[:, c * LANES:(c + 1) * LANES] = (r * epi_scale).astype(o_ref.dtype)
    elif epi == "residual":
        o_ref[...] = (res_ref[...] + gate_ref[0] * acc).astype(o_ref.dtype)


def _group_spec(arr, tm, rows_per_group, width, col_fn):
    g, r, _ = arr.shape
    if r == 1:
        return pl.BlockSpec((1, 1, width),
                            lambda i, j: ((i * tm) // rows_per_group, 0, col_fn(j)))
    return pl.BlockSpec((1, r, width), lambda i, j: (0, 0, col_fn(j)))


def _table_spec(tab, tm, rows_per_group):
    if tab.shape[0] == 1:
        return pl.BlockSpec((1, LANES), lambda i, j: (0, 0))
    nblk = rows_per_group // tm
    return pl.BlockSpec((tm, LANES), lambda i, j: (i % nblk, 0))


def _linear(x, w, *, tm, tn, out_dtype, rows_per_group, norm=None, epi=None,
            epi_args=(), epi_scale=1.0, name="linear"):
    m, k = x.shape
    n = w.shape[1]
    has_norm = norm is not None
    args = [x]
    specs = [pl.BlockSpec((tm, k), lambda i, j: (i, 0))]
    if has_norm:
        gain, shift, scale = norm
        args += [gain.reshape(1, k), shift, scale]
        specs += [pl.BlockSpec((1, k), lambda i, j: (0, 0)),
                  _group_spec(shift, tm, rows_per_group, k, lambda j: 0),
                  _group_spec(scale, tm, rows_per_group, k, lambda j: 0)]
    args.append(w)
    specs.append(pl.BlockSpec((k, tn), lambda i, j: (0, j)))
    if epi in ("rope_ret", "rope_diff"):
        for tab in epi_args:
            args.append(tab)
            specs.append(_table_spec(tab, tm, rows_per_group))
    elif epi == "residual":
        res, gate = epi_args
        args += [res, gate]
        specs += [pl.BlockSpec((tm, tn), lambda i, j: (i, j)),
                  _group_spec(gate, tm, rows_per_group, tn, lambda j: j)]
    kern = functools.partial(_linear_kernel, has_norm=has_norm, epi=epi,
                             epi_scale=epi_scale)
    return pl.pallas_call(
        kern,
        out_shape=jax.ShapeDtypeStruct((m, n), out_dtype),
        grid=(m // tm, pl.cdiv(n, tn)),
        in_specs=specs,
        out_specs=pl.BlockSpec((tm, tn), lambda i, j: (i, j)),
        scratch_shapes=[pltpu.VMEM((tm, k), BF16)] if has_norm else [],
        compiler_params=_params(("parallel", "arbitrary")),
        name=name,
    )(*args)


def _ffn_up_kernel(*refs, decode):
    if decode:
        (x_ref, gain_ref, shift_ref, scale_ref, wa_ref, wg_ref, buf_ref, cw_ref,
         cb_ref, a_ref, u_ref, h_ref) = refs
    else:
        (x_ref, gain_ref, shift_ref, scale_ref, wa_ref, wg_ref,
         a_ref, g_ref, tail_ref, h_ref) = refs

    @pl.when(pl.program_id(1) == 0)
    def _():
        _ada_norm_to(h_ref, x_ref, gain_ref, shift_ref, scale_ref)

    h = h_ref[...]
    a = jnp.dot(h, wa_ref[...], preferred_element_type=F32)
    g = jnp.dot(h, wg_ref[...], preferred_element_type=F32)
    if decode:
        conv = (cb_ref[...] + buf_ref[0] * cw_ref[0:1, :] + buf_ref[1] * cw_ref[1:2, :]
                + a * cw_ref[2:3, :])
        a_ref[...] = a
        u_ref[...] = (_silu(conv) * g).astype(BF16)
    else:
        a_ref[...] = a.astype(BF16)
        g_ref[...] = g.astype(BF16)
        tail_ref[0] = a[a.shape[0] - SUBLANES:, :]


def _ffn_up(x, norm, wa, wg, *, tm, tn, rows_per_group, decode_args=None, name="ffn_up"):
    m, k = x.shape
    f = wa.shape[1]
    gain, shift, scale = norm
    decode = decode_args is not None
    args = [x, gain.reshape(1, k), shift, scale, wa, wg]
    specs = [pl.BlockSpec((tm, k), lambda i, j: (i, 0)),
             pl.BlockSpec((1, k), lambda i, j: (0, 0)),
             _group_spec(shift, tm, rows_per_group, k, lambda j: 0),
             _group_spec(scale, tm, rows_per_group, k, lambda j: 0),
             pl.BlockSpec((k, tn), lambda i, j: (0, j)),
             pl.BlockSpec((k, tn), lambda i, j: (0, j))]
    tile = pl.BlockSpec((tm, tn), lambda i, j: (i, j))
    if decode:
        buf, cw, cb = decode_args
        args += [buf, cw, cb]
        specs += [pl.BlockSpec((CONV_W - 1, tm, tn), lambda i, j: (0, i, j)),
                  pl.BlockSpec((CONV_W, tn), lambda i, j: (0, j)),
                  pl.BlockSpec((1, tn), lambda i, j: (0, j))]
        out_shape = [jax.ShapeDtypeStruct((m, f), F32), jax.ShapeDtypeStruct((m, f), BF16)]
        out_specs = [tile, tile]
    else:
        out_shape = [jax.ShapeDtypeStruct((m, f), BF16), jax.ShapeDtypeStruct((m, f), BF16),
                     jax.ShapeDtypeStruct((m // tm, SUBLANES, f), F32)]
        out_specs = [tile, tile, pl.BlockSpec((1, SUBLANES, tn), lambda i, j: (i, 0, j))]
    return pl.pallas_call(
        functools.partial(_ffn_up_kernel, decode=decode),
        out_shape=out_shape,
        grid=(m // tm, pl.cdiv(f, tn)),
        in_specs=specs,
        out_specs=out_specs,
        scratch_shapes=[pltpu.VMEM((tm, k), BF16)],
        compiler_params=_params(("parallel", "arbitrary")),
        name=name,
    )(*args)


HALO_ROWS = 16


def _conv_gate_kernel(a_ref, halo_ref, g_ref, cw_ref, cb_ref, u_ref):
    tl, f = a_ref.shape
    first = pl.program_id(1) == 0
    row = lax.broadcasted_iota(jnp.int32, (tl, LANES), 0)
    for c in range(f // LANES):
        cols = slice(c * LANES, (c + 1) * LANES)
        a = a_ref[:, cols].astype(F32)
        halo = jnp.where(first, 0.0, halo_ref[:, cols].astype(F32))
        prev1 = halo[HALO_ROWS - 1:HALO_ROWS, :]
        prev2 = halo[HALO_ROWS - 2:HALO_ROWS - 1, :]
        a_m1 = jnp.where(row == 0, prev1, pltpu.roll(a, 1, 0))
        a_m2 = jnp.where(row == 0, prev2, jnp.where(row == 1, prev1, pltpu.roll(a, 2, 0)))
        conv = (cb_ref[:, cols] + a_m2 * cw_ref[0:1, cols] + a_m1 * cw_ref[1:2, cols]
                + a * cw_ref[2:3, cols])
        u_ref[:, cols] = (_silu(conv) * g_ref[:, cols].astype(F32)).astype(BF16)


def _conv_gate(a, g, cw, cb, *, batch, seq, tl=256):
    m, f = a.shape
    nl = seq // tl
    hb = tl // HALO_ROWS
    return pl.pallas_call(
        _conv_gate_kernel,
        out_shape=jax.ShapeDtypeStruct((m, f), BF16),
        grid=(batch, nl),
        in_specs=[pl.BlockSpec((tl, f), lambda b, l: (b * nl + l, 0)),
                  pl.BlockSpec((HALO_ROWS, f),
                               lambda b, l: (jnp.maximum((b * nl + l) * hb - 1, 0), 0)),
                  pl.BlockSpec((tl, f), lambda b, l: (b * nl + l, 0)),
                  pl.BlockSpec((CONV_W, f), lambda b, l: (0, 0)),
                  pl.BlockSpec((1, f), lambda b, l: (0, 0))],
        out_specs=pl.BlockSpec((tl, f), lambda b, l: (b * nl + l, 0)),
        compiler_params=_params(("parallel", "parallel")),
        name="conv_gate",
    )(a, a, g, cw, cb)


def _ret_prompt_kernel(sdec_ref, q_ref, k_ref, v_ref, g_ref, decay_ref, qdec_ref,
                       kdec_ref, o_ref, st_ref, *, chunk):
    sdec = sdec_ref[pl.program_id(1)]
    decay, qdec, kdec = decay_ref[0], qdec_ref[0], kdec_ref[0]
    st_ref[...] = jnp.zeros_like(st_ref)
    for n in range(q_ref.shape[0] // chunk):
        rows = slice(n * chunk, (n + 1) * chunk)
        s = st_ref[0, 0, 0]
        q, k, v = q_ref[rows, :], k_ref[rows, :], v_ref[rows, :]
        att = _dot_nt(q, k) * decay
        o = (jnp.dot(att.astype(BF16), v, preferred_element_type=F32)
             + jnp.dot(q, s.astype(BF16), preferred_element_type=F32) * qdec)
        kd = (k.astype(F32) * kdec).astype(BF16)
        st_ref[0, 0, 0] = s * sdec + _dot_tn(kd, v)
        o_ref[rows, :] = (_rms(o) * _silu(g_ref[rows, :].astype(F32))).astype(BF16)


def _ret_tables(chunk, nh):
    log_g = jnp.log1p(-jnp.exp2(-5.0 - jnp.arange(nh, dtype=F32)))
    idx = jnp.arange(chunk, dtype=F32)
    rel = idx[:, None] - idx[None, :]
    decay = jnp.where(rel >= 0, jnp.exp(jnp.maximum(rel, 0.0)[None] * log_g[:, None, None]), 0.0)
    q_dec = jnp.exp((idx[None, :] + 1.0) * log_g[:, None])[:, :, None]
    k_dec = jnp.exp((chunk - 1.0 - idx)[None, :] * log_g[:, None])[:, :, None]
    s_dec = jnp.exp(chunk * log_g)
    return decay, q_dec, k_dec, s_dec


def _retention_prompt(q, k, v, g, *, batch, seq, nh, dk, dv, chunk):
    m = q.shape[0]
    decay, q_dec, k_dec, s_dec = _ret_tables(chunk, nh)
    seq_head = lambda b, h: (b, h)
    head = lambda b, h: (h, 0, 0)
    return pl.pallas_call(
        functools.partial(_ret_prompt_kernel, chunk=chunk),
        out_shape=[jax.ShapeDtypeStruct((m, nh * dv), BF16),
                   jax.ShapeDtypeStruct((1, batch, nh, dk, dv), F32)],
        grid=(batch, nh),
        in_specs=[pl.BlockSpec(memory_space=pltpu.SMEM),
                  pl.BlockSpec((seq, dk), seq_head),
                  pl.BlockSpec((seq, dk), seq_head),
                  pl.BlockSpec((seq, dv), seq_head),
                  pl.BlockSpec((seq, dv), seq_head),
                  pl.BlockSpec((1, chunk, chunk), head),
                  pl.BlockSpec((1, chunk, 1), head),
                  pl.BlockSpec((1, chunk, 1), head)],
        out_specs=[pl.BlockSpec((seq, dv), seq_head),
                   pl.BlockSpec((1, 1, 1, dk, dv), lambda b, h: (0, b, h, 0, 0))],
        compiler_params=_params(("parallel", "parallel")),
        name="retention_prompt",
    )(s_dec, q, k, v, g, decay, q_dec, k_dec)


def _ret_decode_kernel(q_ref, k_ref, v_ref, g_ref, st_ref, dec_ref, o_ref, sto_ref, *, nh):
    q, k, v = q_ref[0], k_ref[0], v_ref[0]
    decay, qdec, kdec, sdec = (dec_ref[:, i:i + 1] for i in range(4))
    att = jnp.sum(q * k, axis=-1, keepdims=True) * decay
    rows = lax.broadcasted_iota(jnp.int32, (nh, 1), 0)
    qb, vb = q.astype(BF16), v.astype(BF16)
    kd = k * kdec
    qs = jnp.zeros(v.shape, F32)
    for h in range(nh):
        s = st_ref[0, 0, h]
        qs = jnp.where(rows == h, jnp.dot(qb, s.astype(BF16), preferred_element_type=F32), qs)
        k_h = jnp.where(rows == h, kd, 0.0).astype(BF16)
        sto_ref[0, 0, h] = s * sdec[h:h + 1, :] + _dot_tn(k_h, vb)
    o = att * v + qs * qdec
    o_ref[0] = _rms(o) * _silu(g_ref[0])


def _retention_decode(q, k, v, g, state, *, nh, dk, dv):
    b = q.shape[0]
    decay, q_dec, k_dec, s_dec = _ret_tables(1, nh)
    dec = jnp.stack([decay[:, 0, 0], q_dec[:, 0, 0], k_dec[:, 0, 0], s_dec], axis=1)
    tok = lambda w: pl.BlockSpec((1, nh, w), lambda i: (i, 0, 0))
    st = pl.BlockSpec((1, 1, nh, dk, dv), lambda i: (0, i, 0, 0, 0))
    return pl.pallas_call(
        functools.partial(_ret_decode_kernel, nh=nh),
        out_shape=[jax.ShapeDtypeStruct((b, nh, dv), F32),
                   jax.ShapeDtypeStruct(state.shape, F32)],
        grid=(b,),
        in_specs=[tok(dk), tok(dk), tok(dv), tok(dv), st,
                  pl.BlockSpec((nh, 4), lambda i: (0, 0))],
        out_specs=[tok(dv), st],
        compiler_params=_params(("parallel",)),
        name="retention_decode",
    )(q.reshape(b, nh, dk), k.reshape(b, nh, dk), v.reshape(b, nh, dv),
      g.reshape(b, nh, dv), state, dec)


def _diff_out(o, subln, out_scale):
    return _rms(o) * subln * out_scale


def _lane_tile_reduce(x, op):
    out = x[:, :LANES]
    for j in range(1, x.shape[1] // LANES):
        out = op(out, x[:, j * LANES:(j + 1) * LANES])
    return out


def _diff_prompt_kernel(lam_ref, q_ref, k_ref, v_ref, subln_ref, o_ref,
                        kb_ref, vb_ref, s_ref, p_ref, *, tq, dh):
    kb_ref[...] = k_ref[...].astype(BF16)
    vb_ref[...] = v_ref[...].astype(BF16)
    row = lax.broadcasted_iota(jnp.int32, (tq, tq), 0)
    col = lax.broadcasted_iota(jnp.int32, (tq, tq), 1)
    for qi in range(q_ref.shape[0] // tq):
        rows = slice(qi * tq, (qi + 1) * tq)
        kv_len = (qi + 1) * tq
        outs = []
        for c in range(2):
            comp = slice(c * dh, (c + 1) * dh)
            q = q_ref[rows, comp]
            m_part = None
            for ki in range(qi + 1):
                keys = slice(ki * tq, (ki + 1) * tq)
                s = _dot_nt(q, kb_ref[keys, comp])
                if ki == qi:
                    s = jnp.where(col <= row, s, -jnp.inf)
                s_ref[c, :, keys] = s
                blk_max = _lane_tile_reduce(s, jnp.maximum)
                m_part = blk_max if m_part is None else jnp.maximum(m_part, blk_max)
            m = jnp.max(m_part, axis=-1, keepdims=True)
            l_part = jnp.zeros((tq, LANES), F32)
            for ki in range(qi + 1):
                keys = slice(ki * tq, (ki + 1) * tq)
                p = jnp.exp(s_ref[c, :, keys] - m)
                l_part = l_part + _lane_tile_reduce(p, jnp.add)
                p_ref[c, :, keys] = p.astype(BF16)
            l = jnp.sum(l_part, axis=-1, keepdims=True)
            pv = jnp.dot(p_ref[c, :, :kv_len], vb_ref[:kv_len, :], preferred_element_type=F32)
            outs.append(pv / l)
        o = outs[0] - lam_ref[0] * outs[1]
        o_ref[rows, :] = _diff_out(o, subln_ref[...], lam_ref[1]).astype(BF16)


def _diff_attention_prompt(q, k, v, lam, subln, *, batch, seq, nh, dh, dv, tq=256):
    m = q.shape[0]
    seq_head = lambda b, h: (b, h)
    return pl.pallas_call(
        functools.partial(_diff_prompt_kernel, tq=tq, dh=dh),
        out_shape=jax.ShapeDtypeStruct((m, nh * dv), BF16),
        grid=(batch, nh),
        in_specs=[pl.BlockSpec(memory_space=pltpu.SMEM),
                  pl.BlockSpec((seq, 2 * dh), seq_head),
                  pl.BlockSpec((seq, 2 * dh), seq_head),
                  pl.BlockSpec((seq, dv), seq_head),
                  pl.BlockSpec((1, dv), lambda b, h: (0, 0))],
        out_specs=pl.BlockSpec((seq, dv), seq_head),
        scratch_shapes=[pltpu.VMEM((seq, 2 * dh), BF16), pltpu.VMEM((seq, dv), BF16),
                        pltpu.VMEM((2, tq, seq), F32), pltpu.VMEM((2, tq, seq), BF16)],
        compiler_params=_params(("parallel", "parallel")),
        name="diff_attention_prompt",
    )(lam, q, k, v, subln.reshape(1, dv))


PAGES_PER_STEP = 4


def _diff_decode_kernel(pt_ref, lam_ref, q_ref, kn_ref, vn_ref, subln_ref, *rest, nh, page):
    del pt_ref
    ck_refs, cv_refs = rest[:PAGES_PER_STEP], rest[PAGES_PER_STEP:2 * PAGES_PER_STEP]
    o_ref, m_ref, l_ref, acc_ref = rest[2 * PAGES_PER_STEP:]
    p = pl.program_id(1)
    rows_kv = page * nh

    @pl.when(p == 0)
    def _():
        m_ref[...] = jnp.full_like(m_ref, -jnp.inf)
        l_ref[...] = jnp.zeros_like(l_ref)
        acc_ref[...] = jnp.zeros_like(acc_ref)

    q = q_ref[0]
    qb = q.astype(BF16)
    row = lax.broadcasted_iota(jnp.int32, (2 * nh, rows_kv), 0)
    col = lax.broadcasted_iota(jnp.int32, (2 * nh, rows_kv), 1)
    own_head = (col % nh) == (row % nh)
    scores = []
    for ck_ref in ck_refs:
        k0 = ck_ref[0, pl.ds(0, rows_kv, stride=2), :].astype(BF16)
        k1 = ck_ref[0, pl.ds(1, rows_kv, stride=2), :].astype(BF16)
        s = jnp.concatenate([_dot_nt(qb[:nh], k0), _dot_nt(qb[nh:], k1)], axis=0)
        scores.append(jnp.where(own_head, s, -jnp.inf))
    m_old = m_ref[...]
    m_new = jnp.maximum(m_old, jnp.max(functools.reduce(jnp.maximum, scores), axis=-1, keepdims=True))
    alpha = jnp.exp(m_old - m_new)
    l = alpha * l_ref[...]
    acc = alpha * acc_ref[...]
    for s, cv_ref in zip(scores, cv_refs):
        pr = jnp.exp(s - m_new)
        l = l + jnp.sum(pr, axis=-1, keepdims=True)
        acc = acc + jnp.dot(pr.astype(BF16), cv_ref[0].astype(BF16), preferred_element_type=F32)
    m_ref[...] = m_new
    l_ref[...] = l
    acc_ref[...] = acc

    @pl.when(p == pl.num_programs(1) - 1)
    def _():
        s_new = jnp.sum(q * kn_ref[0], axis=-1, keepdims=True)
        m_old = m_ref[...]
        m_new = jnp.maximum(m_old, s_new)
        alpha = jnp.exp(m_old - m_new)
        pn = jnp.exp(s_new - m_new)
        l = alpha * l_ref[...] + pn
        vn = vn_ref[0]
        acc = alpha * acc_ref[...] + pn * jnp.concatenate([vn, vn], axis=0)
        on = acc / l
        o = on[:nh] - lam_ref[0] * on[nh:]
        o_ref[0] = _diff_out(o, subln_ref[...], lam_ref[1])


def _diff_attention_decode(q, k_new, v_new, cache_k, cache_v, page_table, lam, subln,
                           *, nh, dh, dv):
    b = q.shape[0]
    n_pool, page = cache_k.shape[0], cache_k.shape[1]
    n_pages = page_table.shape[1]
    ck = cache_k.reshape(n_pool, page * nh * 2, dh)
    cv = cache_v.reshape(n_pool, page * nh, dv)
    comp_major = lambda t: t.reshape(b, nh, 2, dh).transpose(0, 2, 1, 3).reshape(b, 2 * nh, dh)
    tok = lambda r, w: pl.BlockSpec((1, r, w), lambda i, p, pt: (i, 0, 0))

    def page_spec(rows, width, r):
        return pl.BlockSpec((1, rows, width),
                            lambda i, p, pt: (pt[i * n_pages + p * PAGES_PER_STEP + r], 0, 0))

    steps = n_pages // PAGES_PER_STEP
    grid_spec = pltpu.PrefetchScalarGridSpec(
        num_scalar_prefetch=1,
        grid=(b, steps),
        in_specs=([pl.BlockSpec(memory_space=pltpu.SMEM),
                   tok(2 * nh, dh), tok(2 * nh, dh), tok(nh, dv),
                   pl.BlockSpec((1, dv), lambda i, p, pt: (0, 0))]
                  + [page_spec(page * nh * 2, dh, r) for r in range(PAGES_PER_STEP)]
                  + [page_spec(page * nh, dv, r) for r in range(PAGES_PER_STEP)]),
        out_specs=tok(nh, dv),
        scratch_shapes=[pltpu.VMEM((2 * nh, 1), F32), pltpu.VMEM((2 * nh, 1), F32),
                        pltpu.VMEM((2 * nh, dv), F32)],
    )
    return pl.pallas_call(
        functools.partial(_diff_decode_kernel, nh=nh, page=page),
        out_shape=jax.ShapeDtypeStruct((b, nh, dv), F32),
        grid_spec=grid_spec,
        compiler_params=_params(("parallel", "arbitrary")),
        name="diff_attention_decode",
    )(page_table.reshape(-1), lam, comp_major(q), comp_major(k_new),
      v_new.reshape(b, nh, dv), subln.reshape(1, dv),
      *([ck] * PAGES_PER_STEP), *([cv] * PAGES_PER_STEP))


def _final_norm_kernel(x_ref, gain_ref, o_ref):
    o_ref[...] = _rms(x_ref[...]) * gain_ref[...]


def _final_norm(x, gain, tm):
    m, d = x.shape
    return pl.pallas_call(
        _final_norm_kernel,
        out_shape=jax.ShapeDtypeStruct((m, d), F32),
        grid=(m // tm,),
        in_specs=[pl.BlockSpec((tm, d), lambda i: (i, 0)),
                  pl.BlockSpec((1, d), lambda i: (0, 0))],
        out_specs=pl.BlockSpec((tm, d), lambda i: (i, 0)),
        compiler_params=_params(("parallel",)),
        name="final_norm",
    )(x, gain.reshape(1, d))


def _ret_rope_tables(pos, dk):
    half = dk // 2
    inv = jnp.float32(RET_THETA) ** (-jnp.arange(half, dtype=F32) / half)
    ang = pos.astype(F32)[:, None] * inv[None, :]
    return jnp.cos(ang), jnp.sin(ang)


def _diff_rope_tables(pos, dh, rope_dim):
    half = rope_dim // 2
    inv = jnp.float32(ROPE_THETA) ** (-jnp.arange(half, dtype=F32) / half)
    ang = pos.astype(F32)[:, None] * inv[None, :]
    cos, sin = jnp.cos(ang), jnp.sin(ang)
    n = pos.shape[0]
    rest = dh - rope_dim
    coef = jnp.concatenate([cos, cos, jnp.ones((n, rest), F32)], axis=1)
    from_lo = jnp.concatenate([jnp.zeros((n, half), F32), sin, jnp.zeros((n, rest), F32)], axis=1)
    from_hi = jnp.concatenate([-sin, jnp.zeros((n, half + rest), F32)], axis=1)
    return coef, from_lo, from_hi


def kernel(x_prompt, x_sample, state_ret, state_conv, cache_k, cache_v, page_table, c_prompt, c_sample, ret_wq, ret_wk, ret_wv, ret_wg, ret_wo, kv_norm, kv_wmod, kv_bmod, kv_wk, kv_wv, diff_wq, diff_lq1, diff_lk1, diff_lq2, diff_lk2, diff_subln, diff_wo, norm_mix, norm_ffn, w_mod, b_mod, ffn_wup, ffn_conv, ffn_conv_b, ffn_wdown, norm_final):
    bp, lp, d = x_prompt.shape
    bs, ls, _ = x_sample.shape
    assert ls == 1, "decode group carries one new token per sequence"
    depth = w_mod.shape[0]
    n_a = ret_wq.shape[0]
    nh_r, dk_r, dv_r = state_ret.shape[2], state_ret.shape[3], state_ret.shape[4]
    nh_d, dh_d, dv_d = cache_k.shape[2], cache_k.shape[4], cache_v.shape[3]
    rope_dim = dh_d // 4
    d_ff = ffn_conv.shape[2]
    past_len = page_table.shape[1] * cache_k.shape[1]
    mp = bp * lp
    tm = 1024

    pos_p = jnp.arange(lp)
    pos_s = past_len + jnp.arange(ls)
    bf = lambda w: w.astype(BF16)

    xp = x_prompt.reshape(mp, d)
    xs = x_sample.reshape(bs, d)

    n_c = bp + bs
    n_c_pad = -(-n_c // SUBLANES) * SUBLANES
    c_all = jnp.concatenate([c_prompt, c_sample, jnp.zeros((n_c_pad - n_c, d), F32)], axis=0)

    def split_mod(mod, n):
        mod_p = mod[:bp].reshape(bp, 1, n, d)
        mod_s = mod[bp:n_c].reshape(1, bs, n, d)
        return ([mod_p[:, :, i] for i in range(n)], [mod_s[:, :, i] for i in range(n)])

    lin_p = functools.partial(_linear, tm=tm, rows_per_group=lp)
    lin_s = functools.partial(_linear, tm=bs, rows_per_group=bs)

    ret_p, ret_s, conv_p, conv_s = [], [], [], []
    k_p = v_p = k_s = v_s = None
    for l in range(depth):
        (sm_p, cm_p, gm_p, sf_p, cf_p, gf_p), (sm_s, cm_s, gm_s, sf_s, cf_s, gf_s) = split_mod(
            _modulation(c_all, w_mod, b_mod, l), 6)
        nrm_p = (norm_mix[l], sm_p, cm_p)
        nrm_s = (norm_mix[l], sm_s, cm_s)
        if l < n_a:
            wq, wk, wv, wg, wo = bf(ret_wq[l]), bf(ret_wk[l]), bf(ret_wv[l]), bf(ret_wg[l]), bf(ret_wo[l])
            k_scale = dk_r ** -0.5
            tabs_p = _ret_rope_tables(pos_p, dk_r)
            tabs_s = _ret_rope_tables(pos_s, dk_r)
            q = lin_p(xp, wq, tn=512, out_dtype=BF16, norm=nrm_p, epi="rope_ret", epi_args=tabs_p, name="ret_q_p")
            k = lin_p(xp, wk, tn=512, out_dtype=BF16, norm=nrm_p, epi="rope_ret", epi_args=tabs_p,
                      epi_scale=k_scale, name="ret_k_p")
            v = lin_p(xp, wv, tn=512, out_dtype=BF16, norm=nrm_p, name="ret_v_p")
            g = lin_p(xp, wg, tn=512, out_dtype=BF16, norm=nrm_p, name="ret_g_p")
            o, st_p = _retention_prompt(q, k, v, g, batch=bp, seq=lp, nh=nh_r, dk=dk_r, dv=dv_r,
                                        chunk=RET_CHUNK)
            xp = lin_p(o, wo, tn=512, out_dtype=F32, epi="residual", epi_args=(xp, gm_p), name="ret_o_p")

            q = lin_s(xs, wq, tn=512, out_dtype=F32, norm=nrm_s, epi="rope_ret", epi_args=tabs_s, name="ret_q_s")
            k = lin_s(xs, wk, tn=512, out_dtype=F32, norm=nrm_s, epi="rope_ret", epi_args=tabs_s,
                      epi_scale=k_scale, name="ret_k_s")
            v = lin_s(xs, wv, tn=512, out_dtype=F32, norm=nrm_s, name="ret_v_s")
            g = lin_s(xs, wg, tn=512, out_dtype=F32, norm=nrm_s, name="ret_g_s")
            o, st_s = _retention_decode(q, k, v, g, state_ret[l:l + 1], nh=nh_r, dk=dk_r, dv=dv_r)
            xs = lin_s(o.reshape(bs, nh_r * dv_r), wo, tn=512, out_dtype=F32, epi="residual",
                       epi_args=(xs, gm_s), name="ret_o_s")
            ret_p.append(st_p[0])
            ret_s.append(st_s[0])
        else:
            bi = l - n_a
            tabs_p = _diff_rope_tables(pos_p, dh_d, rope_dim)
            tabs_s = _diff_rope_tables(pos_s, dh_d, rope_dim)
            if bi == 0:
                (sh_p, sc_p), (sh_s, sc_s) = split_mod(
                    _modulation(c_all, kv_wmod[None], kv_bmod[None], 0), 2)
                wk, wv = bf(kv_wk), bf(kv_wv)
                k_p = lin_p(xp, wk, tn=512, out_dtype=F32, norm=(kv_norm, sh_p, sc_p), epi="rope_diff",
                            epi_args=tabs_p, name="kv_k_p")
                v_p = lin_p(xp, wv, tn=512, out_dtype=F32, norm=(kv_norm, sh_p, sc_p), name="kv_v_p")
                k_s = lin_s(xs, wk, tn=512, out_dtype=F32, norm=(kv_norm, sh_s, sc_s), epi="rope_diff",
                            epi_args=tabs_s, name="kv_k_s")
                v_s = lin_s(xs, wv, tn=512, out_dtype=F32, norm=(kv_norm, sh_s, sc_s), name="kv_v_s")
            lam_init = 0.8 - 0.6 * math.exp(-0.3 * l)
            lam = (jnp.exp(jnp.sum(diff_lq1[bi].astype(F32) * diff_lk1[bi].astype(F32)))
                   - jnp.exp(jnp.sum(diff_lq2[bi].astype(F32) * diff_lk2[bi].astype(F32))) + lam_init)
            lam_sc = jnp.stack([lam, jnp.float32(1.0 - lam_init)])
            wq, wo = bf(diff_wq[bi]), bf(diff_wo[bi])
            q_scale = dh_d ** -0.5
            q = lin_p(xp, wq, tn=512, out_dtype=BF16, norm=nrm_p, epi="rope_diff", epi_args=tabs_p,
                      epi_scale=q_scale, name="diff_q_p")
            o = _diff_attention_prompt(q, k_p, v_p, lam_sc, diff_subln[bi], batch=bp, seq=lp, nh=nh_d,
                                       dh=dh_d, dv=dv_d)
            xp = lin_p(o, wo, tn=512, out_dtype=F32, epi="residual", epi_args=(xp, gm_p), name="diff_o_p")

            q = lin_s(xs, wq, tn=512, out_dtype=F32, norm=nrm_s, epi="rope_diff", epi_args=tabs_s,
                      epi_scale=q_scale, name="diff_q_s")
            o = _diff_attention_decode(q, k_s, v_s, cache_k, cache_v, page_table, lam_sc, diff_subln[bi],
                                       nh=nh_d, dh=dh_d, dv=dv_d)
            xs = lin_s(o.reshape(bs, nh_d * dv_d), wo, tn=512, out_dtype=F32, epi="residual",
                       epi_args=(xs, gm_s), name="diff_o_s")

        wa, wg_ = bf(ffn_wup[l][:, :d_ff]), bf(ffn_wup[l][:, d_ff:])
        wdn = bf(ffn_wdown[l])
        cw, cb = ffn_conv[l], ffn_conv_b[l].reshape(1, d_ff)
        a, g, tails = _ffn_up(xp, (norm_ffn[l], sf_p, cf_p), wa, wg_, tm=tm, tn=512, rows_per_group=lp,
                              name="ffn_up_p")
        u = _conv_gate(a, g, cw, cb, batch=bp, seq=lp)
        xp = lin_p(u, wdn, tn=512, out_dtype=F32, epi="residual", epi_args=(xp, gf_p), name="ffn_down_p")
        last = tails.reshape(bp, lp // tm, SUBLANES, d_ff)[:, -1, SUBLANES - (CONV_W - 1):, :]
        conv_p.append(last)

        buf = jnp.swapaxes(state_conv[l], 0, 1)
        a_s, u_s = _ffn_up(xs, (norm_ffn[l], sf_s, cf_s), wa, wg_, tm=bs, tn=512, rows_per_group=bs,
                           decode_args=(buf, cw, cb), name="ffn_up_s")
        xs = lin_s(u_s, wdn, tn=512, out_dtype=F32, epi="residual", epi_args=(xs, gf_s), name="ffn_down_s")
        conv_s.append(jnp.concatenate([state_conv[l][:, 1:], a_s[:, None, :]], axis=1))

    y_prompt = _final_norm(xp, norm_final, tm).reshape(bp, lp, d)
    y_sample = _final_norm(xs, norm_final, bs).reshape(bs, ls, d)
    return (y_prompt, y_sample, jnp.stack(ret_p), jnp.stack(ret_s), jnp.stack(conv_p), jnp.stack(conv_s),
            k_p.reshape(bp, lp, nh_d, 2, dh_d), v_p.reshape(bp, lp, nh_d, dv_d),
            k_s.reshape(bs, ls, nh_d, 2, dh_d), v_s.reshape(bs, ls, nh_d, dv_d))
```

```python
import functools
import math

import jax
import jax.numpy as jnp
from jax import lax
from jax.experimental import pallas as pl
from jax.experimental.pallas import tpu as pltpu

F32 = jnp.float32
BF16 = jnp.bfloat16

NORM_EPS = 1e-6
RET_THETA = 10000.0
ROPE_THETA = 500000.0
RET_CHUNK = 256
CONV_W = 3

V7X_VMEM_LIMIT_BYTES = 56 * 1024 * 1024
LANES = 128
SUBLANES = 8


def _params(semantics):
    return pltpu.CompilerParams(dimension_semantics=semantics,
                                vmem_limit_bytes=V7X_VMEM_LIMIT_BYTES)


def _silu(x):
    return x * jax.nn.sigmoid(x)


def _rms(x):
    return x * lax.rsqrt(jnp.mean(x * x, axis=-1, keepdims=True) + NORM_EPS)


def _dot_nt(a, b):
    return lax.dot_general(a, b, (((1,), (1,)), ((), ())), preferred_element_type=F32)


def _dot_tn(a, b):
    return lax.dot_general(a, b, (((0,), (0,)), ((), ())), preferred_element_type=F32)


def _mod_kernel(c_ref, w_ref, b_ref, o_ref, s_ref):
    @pl.when(pl.program_id(0) == 0)
    def _():
        s_ref[...] = _silu(c_ref[...]).astype(BF16)

    acc = jnp.dot(s_ref[...], w_ref[...].astype(BF16), preferred_element_type=F32)
    o_ref[...] = acc + b_ref[...]


def _modulation(c, w, b, layer, tn=512):
    m, d = c.shape
    n = w.shape[2]
    return pl.pallas_call(
        _mod_kernel,
        out_shape=jax.ShapeDtypeStruct((m, n), F32),
        grid=(n // tn,),
        in_specs=[pl.BlockSpec((m, d), lambda j: (0, 0)),
                  pl.BlockSpec((None, d, tn), lambda j: (layer, 0, j)),
                  pl.BlockSpec((None, 1, tn), lambda j: (layer, 0, j))],
        out_specs=pl.BlockSpec((m, tn), lambda j: (0, j)),
        scratch_shapes=[pltpu.VMEM((m, d), BF16)],
        compiler_params=_params(("arbitrary",)),
        name="modulation",
    )(c, w, b.reshape(b.shape[0], 1, n))


NORM_ROWS = 128


def _ada_norm_to(h_ref, x_ref, gain_ref, shift_ref, scale_ref):
    tm = x_ref.shape[0]
    if shift_ref.shape[1] != 1:
        y = _rms(x_ref[...].astype(F32)) * gain_ref[...]
        h_ref[...] = (y * (1.0 + scale_ref[0]) + shift_ref[0]).astype(BF16)
        return

    def body(r, carry):
        rows = pl.ds(pl.multiple_of(r * NORM_ROWS, NORM_ROWS), NORM_ROWS)
        y = _rms(x_ref[rows, :].astype(F32)) * gain_ref[...]
        h_ref[rows, :] = (y * (1.0 + scale_ref[0]) + shift_ref[0]).astype(BF16)
        return carry

    lax.fori_loop(0, tm // NORM_ROWS, body, 0)


def _epilogue(o_ref, acc, epi, epi_scale, extra):
    tn = acc.shape[1]
    if epi is None:
        o_ref[...] = acc.astype(o_ref.dtype)
    elif epi == "rope_ret":
        cos, sin = extra[0][...], extra[1][...]
        for c in range(tn // (2 * LANES)):
            lo, mid, hi = 2 * c * LANES, (2 * c + 1) * LANES, (2 * c + 2) * LANES
            x1, x2 = acc[:, lo:mid], acc[:, mid:hi]
            o_ref[:, lo:mid] = ((x1 * cos - x2 * sin) * epi_scale).astype(o_ref.dtype)
            o_ref[:, mid:hi] = ((x2 * cos + x1 * sin) * epi_scale).astype(o_ref.dtype)
    elif epi == "rope_diff":
        cc, sa, sb = extra[0][...], extra[1][...], extra[2][...]
        for c in range(tn // LANES):
            xc = acc[:, c * LANES:(c + 1) * LANES]
            r = (xc * cc + pltpu.roll(xc, 16, 1) * sa
                 + pltpu.roll(xc, LANES - 16, 1) * sb)
            o_ref[:, c * LANES:(c + 1) * LANES] = (r * epi_scale).astype(o_ref.dtype)
    elif epi == "residual":
        res_ref, gate_ref = extra
        o_ref[...] = (res_ref[...] + gate_ref[0] * acc).astype(o_ref.dtype)


def _take(it, n):
    return [next(it) for _ in range(n)]


def _linear_kernel(*refs, has_norm, epi, epi_scale):
    n_extra = {None: 0, "rope_ret": 2, "rope_diff": 3, "residual": 2}[epi]
    it = iter(refs)
    xp_ref, xs_ref = next(it), next(it)
    if has_norm:
        gain_ref, shp_ref, scp_ref, shs_ref, scs_ref = _take(it, 5)
    w_ref = next(it)
    extra_p, extra_s = _take(it, n_extra), _take(it, n_extra)
    op_ref, os_ref = next(it), next(it)
    i, j = pl.program_id(0), pl.program_id(1)

    if has_norm:
        hp_ref, hs_ref = next(it), next(it)

        @pl.when(j == 0)
        def _():
            _ada_norm_to(hp_ref, xp_ref, gain_ref, shp_ref, scp_ref)

        @pl.when((i == 0) & (j == 0))
        def _():
            _ada_norm_to(hs_ref, xs_ref, gain_ref, shs_ref, scs_ref)

        a = hp_ref[...]
    else:
        a = xp_ref[...].astype(BF16)

    w = w_ref[...].astype(BF16)
    _epilogue(op_ref, jnp.dot(a, w, preferred_element_type=F32), epi, epi_scale, extra_p)

    @pl.when(i == 0)
    def _():
        a_s = hs_ref[...] if has_norm else xs_ref[...].astype(BF16)
        _epilogue(os_ref.at[0], jnp.dot(a_s, w, preferred_element_type=F32), epi, epi_scale,
                  extra_s)

    @pl.when(i != 0)
    def _():
        os_ref[...] = jnp.zeros_like(os_ref)


def _group_spec(arr, tm, rows_per_group, width, col_fn):
    g, r, _ = arr.shape
    if r == 1:
        return pl.BlockSpec((1, 1, width),
                            lambda i, j: ((i * tm) // rows_per_group, 0, col_fn(j)))
    return pl.BlockSpec((1, r, width), lambda i, j: (0, 0, col_fn(j)))


def _table_spec(tab, tm, rows_per_group):
    if tab.shape[0] == 1:
        return pl.BlockSpec((1, LANES), lambda i, j: (0, 0))
    nblk = rows_per_group // tm
    return pl.BlockSpec((tm, LANES), lambda i, j: (i % nblk, 0))


def _decode_out_spec(ms, tn):
    return pl.BlockSpec((1, ms, tn), lambda i, j: (i, 0, j))


def _linear(xp, xs, w, layer, *, tm, tn, out_dtype, rows_per_group, norm=None, epi=None,
            epi_p=(), epi_s=(), epi_scale=1.0, name="linear"):
    m, k = xp.shape
    ms = xs.shape[0]
    n = w.shape[2]
    has_norm = norm is not None
    args = [xp, xs]
    specs = [pl.BlockSpec((tm, k), lambda i, j: (i, 0)),
             pl.BlockSpec((ms, k), lambda i, j: (0, 0))]
    if has_norm:
        gain, shift_p, scale_p, shift_s, scale_s = norm
        args += [gain.reshape(1, k), shift_p, scale_p, shift_s, scale_s]
        specs += [pl.BlockSpec((1, k), lambda i, j: (0, 0)),
                  _group_spec(shift_p, tm, rows_per_group, k, lambda j: 0),
                  _group_spec(scale_p, tm, rows_per_group, k, lambda j: 0),
                  _group_spec(shift_s, tm, rows_per_group, k, lambda j: 0),
                  _group_spec(scale_s, tm, rows_per_group, k, lambda j: 0)]
    args.append(w)
    specs.append(pl.BlockSpec((None, k, tn), lambda i, j: (layer, 0, j)))
    if epi in ("rope_ret", "rope_diff"):
        for tab in tuple(epi_p) + tuple(epi_s):
            args.append(tab)
            specs.append(_table_spec(tab, tm, rows_per_group))
    elif epi == "residual":
        (res_p, gate_p), (res_s, gate_s) = epi_p, epi_s
        args += [res_p, gate_p, res_s, gate_s]
        specs += [pl.BlockSpec((tm, tn), lambda i, j: (i, j)),
                  _group_spec(gate_p, tm, rows_per_group, tn, lambda j: j),
                  pl.BlockSpec((ms, tn), lambda i, j: (0, j)),
                  _group_spec(gate_s, tm, rows_per_group, tn, lambda j: j)]
    out_shape = [jax.ShapeDtypeStruct((m, n), out_dtype),
                 jax.ShapeDtypeStruct((m // tm, ms, n), F32)]
    out_specs = [pl.BlockSpec((tm, tn), lambda i, j: (i, j)), _decode_out_spec(ms, tn)]
    if has_norm:
        out_shape += [jax.ShapeDtypeStruct((m, k), BF16), jax.ShapeDtypeStruct((ms, k), BF16)]
        out_specs += [pl.BlockSpec((tm, k), lambda i, j: (i, 0)),
                      pl.BlockSpec((ms, k), lambda i, j: (0, 0))]
    kern = functools.partial(_linear_kernel, has_norm=has_norm, epi=epi, epi_scale=epi_scale)
    outs = pl.pallas_call(
        kern,
        out_shape=out_shape,
        grid=(m // tm, n // tn),
        in_specs=specs,
        out_specs=out_specs,
        compiler_params=_params(("arbitrary", "arbitrary")),
        name=name,
    )(*args)
    if has_norm:
        return outs[0], outs[1][0], outs[2], outs[3]
    return outs[0], outs[1][0]


def _ffn_up_kernel(xp_ref, xs_ref, gain_ref, shp_ref, scp_ref, shs_ref, scs_ref, wa_ref, wg_ref,
                   buf_ref, cw_ref, cb_ref, a_ref, g_ref, tail_ref, as_ref, us_ref,
                   hp_ref, hs_ref):
    i, j = pl.program_id(0), pl.program_id(1)

    @pl.when(j == 0)
    def _():
        _ada_norm_to(hp_ref, xp_ref, gain_ref, shp_ref, scp_ref)

    @pl.when((i == 0) & (j == 0))
    def _():
        _ada_norm_to(hs_ref, xs_ref, gain_ref, shs_ref, scs_ref)

    wa, wg = wa_ref[...], wg_ref[...]
    h = hp_ref[...]
    a = jnp.dot(h, wa, preferred_element_type=F32)
    a_ref[...] = a.astype(BF16)
    g_ref[...] = jnp.dot(h, wg, preferred_element_type=F32).astype(BF16)
    tail_ref[0] = a[a.shape[0] - SUBLANES:, :]

    @pl.when(i == 0)
    def _():
        hs = hs_ref[...]
        a_s = jnp.dot(hs, wa, preferred_element_type=F32)
        g_s = jnp.dot(hs, wg, preferred_element_type=F32)
        conv = (cb_ref[...] + buf_ref[0] * cw_ref[0:1, :] + buf_ref[1] * cw_ref[1:2, :]
                + a_s * cw_ref[2:3, :])
        as_ref[0] = a_s
        us_ref[0] = (_silu(conv) * g_s).astype(BF16)

    @pl.when(i != 0)
    def _():
        as_ref[...] = jnp.zeros_like(as_ref)
        us_ref[...] = jnp.zeros_like(us_ref)


def _ffn_up(xp, xs, norm, wa, wg, buf, cw, cb, *, tm, tn, rows_per_group, name="ffn_up"):
    m, k = xp.shape
    ms = xs.shape[0]
    f = wa.shape[1]
    gain, shift_p, scale_p, shift_s, scale_s = norm
    args = [xp, xs, gain.reshape(1, k), shift_p, scale_p, shift_s, scale_s, wa, wg, buf, cw, cb]
    specs = [pl.BlockSpec((tm, k), lambda i, j: (i, 0)),
             pl.BlockSpec((ms, k), lambda i, j: (0, 0)),
             pl.BlockSpec((1, k), lambda i, j: (0, 0)),
             _group_spec(shift_p, tm, rows_per_group, k, lambda j: 0),
             _group_spec(scale_p, tm, rows_per_group, k, lambda j: 0),
             _group_spec(shift_s, tm, rows_per_group, k, lambda j: 0),
             _group_spec(scale_s, tm, rows_per_group, k, lambda j: 0),
             pl.BlockSpec((k, tn), lambda i, j: (0, j)),
             pl.BlockSpec((k, tn), lambda i, j: (0, j)),
             pl.BlockSpec((CONV_W - 1, ms, tn), lambda i, j: (0, 0, j)),
             pl.BlockSpec((CONV_W, tn), lambda i, j: (0, j)),
             pl.BlockSpec((1, tn), lambda i, j: (0, j))]
    tile = pl.BlockSpec((tm, tn), lambda i, j: (i, j))
    out_shape = [jax.ShapeDtypeStruct((m, f), BF16), jax.ShapeDtypeStruct((m, f), BF16),
                 jax.ShapeDtypeStruct((m // tm, SUBLANES, f), F32),
                 jax.ShapeDtypeStruct((m // tm, ms, f), F32),
                 jax.ShapeDtypeStruct((m // tm, ms, f), BF16)]
    out_specs = [tile, tile, pl.BlockSpec((1, SUBLANES, tn), lambda i, j: (i, 0, j)),
                 _decode_out_spec(ms, tn), _decode_out_spec(ms, tn)]
    a, g, tails, a_s, u_s = pl.pallas_call(
        _ffn_up_kernel,
        out_shape=out_shape,
        grid=(m // tm, pl.cdiv(f, tn)),
        in_specs=specs,
        out_specs=out_specs,
        scratch_shapes=[pltpu.VMEM((tm, k), BF16), pltpu.VMEM((ms, k), BF16)],
        compiler_params=_params(("arbitrary", "arbitrary")),
        name=name,
    )(*args)
    return a, g, tails, a_s[0], u_s[0]


HALO_ROWS = 16


def _conv_gate_kernel(a_ref, halo_ref, g_ref, cw_ref, cb_ref, u_ref):
    tl, f = a_ref.shape
    first = pl.program_id(1) == 0
    row = lax.broadcasted_iota(jnp.int32, (tl, LANES), 0)
    for c in range(f // LANES):
        cols = slice(c * LANES, (c + 1) * LANES)
        a = a_ref[:, cols].astype(F32)
        halo = jnp.where(first, 0.0, halo_ref[:, cols].astype(F32))
        prev1 = halo[HALO_ROWS - 1:HALO_ROWS, :]
        prev2 = halo[HALO_ROWS - 2:HALO_ROWS - 1, :]
        a_m1 = jnp.where(row == 0, prev1, pltpu.roll(a, 1, 0))
        a_m2 = jnp.where(row == 0, prev2, jnp.where(row == 1, prev1, pltpu.roll(a, 2, 0)))
        conv = (cb_ref[:, cols] + a_m2 * cw_ref[0:1, cols] + a_m1 * cw_ref[1:2, cols]
                + a * cw_ref[2:3, cols])
        u_ref[:, cols] = (_silu(conv) * g_ref[:, cols].astype(F32)).astype(BF16)


def _conv_gate(a, g, cw, cb, *, batch, seq, tl=256):
    m, f = a.shape
    nl = seq // tl
    hb = tl // HALO_ROWS
    return pl.pallas_call(
        _conv_gate_kernel,
        out_shape=jax.ShapeDtypeStruct((m, f), BF16),
        grid=(batch, nl),
        in_specs=[pl.BlockSpec((tl, f), lambda b, l: (b * nl + l, 0)),
                  pl.BlockSpec((HALO_ROWS, f),
                               lambda b, l: (jnp.maximum((b * nl + l) * hb - 1, 0), 0)),
                  pl.BlockSpec((tl, f), lambda b, l: (b * nl + l, 0)),
                  pl.BlockSpec((CONV_W, f), lambda b, l: (0, 0)),
                  pl.BlockSpec((1, f), lambda b, l: (0, 0))],
        out_specs=pl.BlockSpec((tl, f), lambda b, l: (b * nl + l, 0)),
        compiler_params=_params(("parallel", "parallel")),
        name="conv_gate",
    )(a, a, g, cw, cb)


def _ret_prompt_kernel(sdec_ref, q_ref, k_ref, v_ref, g_ref, decay_ref, qdec_ref,
                       kdec_ref, o_ref, st_ref, *, chunk):
    sdec = sdec_ref[pl.program_id(1)]
    decay, qdec, kdec = decay_ref[0], qdec_ref[0], kdec_ref[0]
    st_ref[...] = jnp.zeros_like(st_ref)
    for n in range(q_ref.shape[0] // chunk):
        rows = slice(n * chunk, (n + 1) * chunk)
        s = st_ref[0, 0, 0]
        q, k, v = q_ref[rows, :], k_ref[rows, :], v_ref[rows, :]
        att = _dot_nt(q, k) * decay
        o = (jnp.dot(att.astype(BF16), v, preferred_element_type=F32)
             + jnp.dot(q, s.astype(BF16), preferred_element_type=F32) * qdec)
        kd = (k.astype(F32) * kdec).astype(BF16)
        st_ref[0, 0, 0] = s * sdec + _dot_tn(kd, v)
        o_ref[rows, :] = (_rms(o) * _silu(g_ref[rows, :].astype(F32))).astype(BF16)


def _ret_tables(chunk, nh):
    log_g = jnp.log1p(-jnp.exp2(-5.0 - jnp.arange(nh, dtype=F32)))
    idx = jnp.arange(chunk, dtype=F32)
    rel = idx[:, None] - idx[None, :]
    decay = jnp.where(rel >= 0, jnp.exp(jnp.maximum(rel, 0.0)[None] * log_g[:, None, None]), 0.0)
    q_dec = jnp.exp((idx[None, :] + 1.0) * log_g[:, None])[:, :, None]
    k_dec = jnp.exp((chunk - 1.0 - idx)[None, :] * log_g[:, None])[:, :, None]
    s_dec = jnp.exp(chunk * log_g)
    return decay, q_dec, k_dec, s_dec


def _retention_prompt(q, k, v, g, *, batch, seq, nh, dk, dv, chunk):
    m = q.shape[0]
    decay, q_dec, k_dec, s_dec = _ret_tables(chunk, nh)
    seq_head = lambda b, h: (b, h)
    head = lambda b, h: (h, 0, 0)
    return pl.pallas_call(
        functools.partial(_ret_prompt_kernel, chunk=chunk),
        out_shape=[jax.ShapeDtypeStruct((m, nh * dv), BF16),
                   jax.ShapeDtypeStruct((1, batch, nh, dk, dv), F32)],
        grid=(batch, nh),
        in_specs=[pl.BlockSpec(memory_space=pltpu.SMEM),
                  pl.BlockSpec((seq, dk), seq_head),
                  pl.BlockSpec((seq, dk), seq_head),
                  pl.BlockSpec((seq, dv), seq_head),
                  pl.BlockSpec((seq, dv), seq_head),
                  pl.BlockSpec((1, chunk, chunk), head),
                  pl.BlockSpec((1, chunk, 1), head),
                  pl.BlockSpec((1, chunk, 1), head)],
        out_specs=[pl.BlockSpec((seq, dv), seq_head),
                   pl.BlockSpec((1, 1, 1, dk, dv), lambda b, h: (0, b, h, 0, 0))],
        compiler_params=_params(("parallel", "parallel")),
        name="retention_prompt",
    )(s_dec, q, k, v, g, decay, q_dec, k_dec)


def _ret_decode_kernel(q_ref, k_ref, v_ref, g_ref, st_ref, dec_ref, o_ref, sto_ref, *, nh):
    q, k, v = q_ref[0], k_ref[0], v_ref[0]
    decay, qdec, kdec, sdec = (dec_ref[:, i:i + 1] for i in range(4))
    att = jnp.sum(q * k, axis=-1, keepdims=True) * decay
    rows = lax.broadcasted_iota(jnp.int32, (nh, 1), 0)
    qb, vb = q.astype(BF16), v.astype(BF16)
    kd = k * kdec
    qs = jnp.zeros(v.shape, F32)
    for h in range(nh):
        s = st_ref[0, 0, h]
        qs = jnp.where(rows == h, jnp.dot(qb, s.astype(BF16), preferred_element_type=F32), qs)
        k_h = jnp.where(rows == h, kd, 0.0).astype(BF16)
        sto_ref[0, 0, h] = s * sdec[h:h + 1, :] + _dot_tn(k_h, vb)
    o = att * v + qs * qdec
    o_ref[0] = _rms(o) * _silu(g_ref[0])


def _retention_decode(q, k, v, g, state, *, nh, dk, dv):
    b = q.shape[0]
    decay, q_dec, k_dec, s_dec = _ret_tables(1, nh)
    dec = jnp.stack([decay[:, 0, 0], q_dec[:, 0, 0], k_dec[:, 0, 0], s_dec], axis=1)
    tok = lambda w: pl.BlockSpec((1, nh, w), lambda i: (i, 0, 0))
    st = pl.BlockSpec((1, 1, nh, dk, dv), lambda i: (0, i, 0, 0, 0))
    return pl.pallas_call(
        functools.partial(_ret_decode_kernel, nh=nh),
        out_shape=[jax.ShapeDtypeStruct((b, nh, dv), F32),
                   jax.ShapeDtypeStruct(state.shape, F32)],
        grid=(b,),
        in_specs=[tok(dk), tok(dk), tok(dv), tok(dv), st,
                  pl.BlockSpec((nh, 4), lambda i: (0, 0))],
        out_specs=[tok(dv), st],
        compiler_params=_params(("parallel",)),
        name="retention_decode",
    )(q.reshape(b, nh, dk), k.reshape(b, nh, dk), v.reshape(b, nh, dv),
      g.reshape(b, nh, dv), state, dec)


def _diff_out(o, subln, out_scale):
    return _rms(o) * subln * out_scale


def _lane_tile_reduce(x, op):
    out = x[:, :LANES]
    for j in range(1, x.shape[1] // LANES):
        out = op(out, x[:, j * LANES:(j + 1) * LANES])
    return out


def _diff_prompt_kernel(lam_ref, q_ref, k_ref, v_ref, subln_ref, o_ref,
                        kb_ref, vb_ref, s_ref, p_ref, *, tq, dh):
    kb_ref[...] = k_ref[...].astype(BF16)
    vb_ref[...] = v_ref[...].astype(BF16)
    row = lax.broadcasted_iota(jnp.int32, (tq, tq), 0)
    col = lax.broadcasted_iota(jnp.int32, (tq, tq), 1)
    for qi in range(q_ref.shape[0] // tq):
        rows = slice(qi * tq, (qi + 1) * tq)
        kv_len = (qi + 1) * tq
        outs = []
        for c in range(2):
            comp = slice(c * dh, (c + 1) * dh)
            q = q_ref[rows, comp]
            m_part = None
            for ki in range(qi + 1):
                keys = slice(ki * tq, (ki + 1) * tq)
                s = _dot_nt(q, kb_ref[keys, comp])
                if ki == qi:
                    s = jnp.where(col <= row, s, -jnp.inf)
                s_ref[c, :, keys] = s
                blk_max = _lane_tile_reduce(s, jnp.maximum)
                m_part = blk_max if m_part is None else jnp.maximum(m_part, blk_max)
            m = jnp.max(m_part, axis=-1, keepdims=True)
            l_part = jnp.zeros((tq, LANES), F32)
            for ki in range(qi + 1):
                keys = slice(ki * tq, (ki + 1) * tq)
                p = jnp.exp(s_ref[c, :, keys] - m)
                l_part = l_part + _lane_tile_reduce(p, jnp.add)
                p_ref[c, :, keys] = p.astype(BF16)
            l = jnp.sum(l_part, axis=-1, keepdims=True)
            pv = jnp.dot(p_ref[c, :, :kv_len], vb_ref[:kv_len, :], preferred_element_type=F32)
            outs.append(pv / l)
        o = outs[0] - lam_ref[0] * outs[1]
        o_ref[rows, :] = _diff_out(o, subln_ref[...], lam_ref[1]).astype(BF16)


def _diff_attention_prompt(q, k, v, lam, subln, *, batch, seq, nh, dh, dv, tq=256):
    m = q.shape[0]
    seq_head = lambda b, h: (b, h)
    return pl.pallas_call(
        functools.partial(_diff_prompt_kernel, tq=tq, dh=dh),
        out_shape=jax.ShapeDtypeStruct((m, nh * dv), BF16),
        grid=(batch, nh),
        in_specs=[pl.BlockSpec(memory_space=pltpu.SMEM),
                  pl.BlockSpec((seq, 2 * dh), seq_head),
                  pl.BlockSpec((seq, 2 * dh), seq_head),
                  pl.BlockSpec((seq, dv), seq_head),
                  pl.BlockSpec((1, dv), lambda b, h: (0, 0))],
        out_specs=pl.BlockSpec((seq, dv), seq_head),
        scratch_shapes=[pltpu.VMEM((seq, 2 * dh), BF16), pltpu.VMEM((seq, dv), BF16),
                        pltpu.VMEM((2, tq, seq), F32), pltpu.VMEM((2, tq, seq), BF16)],
        compiler_params=_params(("parallel", "parallel")),
        name="diff_attention_prompt",
    )(lam, q, k, v, subln.reshape(1, dv))


PAGES_PER_STEP = 8


def _diff_decode_kernel(pt_ref, lam_ref, q_ref, kn_ref, vn_ref, subln_ref, *rest, nh, page):
    del pt_ref
    ck_refs, cv_refs = rest[:PAGES_PER_STEP], rest[PAGES_PER_STEP:2 * PAGES_PER_STEP]
    o_ref, m_ref, l_ref, acc_ref = rest[2 * PAGES_PER_STEP:]
    p = pl.program_id(1)
    rows_kv = page * nh

    @pl.when(p == 0)
    def _():
        m_ref[...] = jnp.full_like(m_ref, -jnp.inf)
        l_ref[...] = jnp.zeros_like(l_ref)
        acc_ref[...] = jnp.zeros_like(acc_ref)

    q = q_ref[0]
    qb = q.astype(BF16)
    row = lax.broadcasted_iota(jnp.int32, (2 * nh, rows_kv), 0)
    col = lax.broadcasted_iota(jnp.int32, (2 * nh, rows_kv), 1)
    own_head = (col % nh) == (row % nh)
    scores = []
    for ck_ref in ck_refs:
        k0 = ck_ref[0, pl.ds(0, rows_kv, stride=2), :].astype(BF16)
        k1 = ck_ref[0, pl.ds(1, rows_kv, stride=2), :].astype(BF16)
        s = jnp.concatenate([_dot_nt(qb[:nh], k0), _dot_nt(qb[nh:], k1)], axis=0)
        scores.append(jnp.where(own_head, s, -jnp.inf))
    m_old = m_ref[...]
    m_new = jnp.maximum(m_old, jnp.max(functools.reduce(jnp.maximum, scores), axis=-1, keepdims=True))
    alpha = jnp.exp(m_old - m_new)
    l = alpha * l_ref[...]
    acc = alpha * acc_ref[...]
    for s, cv_ref in zip(scores, cv_refs):
        pr = jnp.exp(s - m_new)
        l = l + jnp.sum(pr, axis=-1, keepdims=True)
        acc = acc + jnp.dot(pr.astype(BF16), cv_ref[0].astype(BF16), preferred_element_type=F32)
    m_ref[...] = m_new
    l_ref[...] = l
    acc_ref[...] = acc

    @pl.when(p == pl.num_programs(1) - 1)
    def _():
        s_new = jnp.sum(q * kn_ref[0], axis=-1, keepdims=True)
        m_old = m_ref[...]
        m_new = jnp.maximum(m_old, s_new)
        alpha = jnp.exp(m_old - m_new)
        pn = jnp.exp(s_new - m_new)
        l = alpha * l_ref[...] + pn
        vn = vn_ref[0]
        acc = alpha * acc_ref[...] + pn * jnp.concatenate([vn, vn], axis=0)
        on = acc / l
        o = on[:nh] - lam_ref[0] * on[nh:]
        o_ref[0] = _diff_out(o, subln_ref[...], lam_ref[1])


def _diff_attention_decode(q, k_new, v_new, cache_k, cache_v, page_table, lam, subln,
                           *, nh, dh, dv):
    b = q.shape[0]
    n_pool, page = cache_k.shape[0], cache_k.shape[1]
    n_pages = page_table.shape[1]
    ck = cache_k.reshape(n_pool, page * nh * 2, dh)
    cv = cache_v.reshape(n_pool, page * nh, dv)
    comp_major = lambda t: t.reshape(b, nh, 2, dh).transpose(0, 2, 1, 3).reshape(b, 2 * nh, dh)
    tok = lambda r, w: pl.BlockSpec((1, r, w), lambda i, p, pt: (i, 0, 0))

    def page_spec(rows, width, r):
        return pl.BlockSpec((1, rows, width),
                            lambda i, p, pt: (pt[i * n_pages + p * PAGES_PER_STEP + r], 0, 0))

    steps = n_pages // PAGES_PER_STEP
    grid_spec = pltpu.PrefetchScalarGridSpec(
        num_scalar_prefetch=1,
        grid=(b, steps),
        in_specs=([pl.BlockSpec(memory_space=pltpu.SMEM),
                   tok(2 * nh, dh), tok(2 * nh, dh), tok(nh, dv),
                   pl.BlockSpec((1, dv), lambda i, p, pt: (0, 0))]
                  + [page_spec(page * nh * 2, dh, r) for r in range(PAGES_PER_STEP)]
                  + [page_spec(page * nh, dv, r) for r in range(PAGES_PER_STEP)]),
        out_specs=tok(nh, dv),
        scratch_shapes=[pltpu.VMEM((2 * nh, 1), F32), pltpu.VMEM((2 * nh, 1), F32),
                        pltpu.VMEM((2 * nh, dv), F32)],
    )
    return pl.pallas_call(
        functools.partial(_diff_decode_kernel, nh=nh, page=page),
        out_shape=jax.ShapeDtypeStruct((b, nh, dv), F32),
        grid_spec=grid_spec,
        compiler_params=_params(("parallel", "arbitrary")),
        name="diff_attention_decode",
    )(page_table.reshape(-1), lam, comp_major(q), comp_major(k_new),
      v_new.reshape(b, nh, dv), subln.reshape(1, dv),
      *([ck] * PAGES_PER_STEP), *([cv] * PAGES_PER_STEP))


def _final_norm_kernel(x_ref, gain_ref, o_ref):
    o_ref[...] = _rms(x_ref[...]) * gain_ref[...]


def _final_norm(x, gain, tm):
    m, d = x.shape
    return pl.pallas_call(
        _final_norm_kernel,
        out_shape=jax.ShapeDtypeStruct((m, d), F32),
        grid=(m // tm,),
        in_specs=[pl.BlockSpec((tm, d), lambda i: (i, 0)),
                  pl.BlockSpec((1, d), lambda i: (0, 0))],
        out_specs=pl.BlockSpec((tm, d), lambda i: (i, 0)),
        compiler_params=_params(("parallel",)),
        name="final_norm",
    )(x, gain.reshape(1, d))


def _ret_rope_tables(pos, dk):
    half = dk // 2
    inv = jnp.float32(RET_THETA) ** (-jnp.arange(half, dtype=F32) / half)
    ang = pos.astype(F32)[:, None] * inv[None, :]
    return jnp.cos(ang), jnp.sin(ang)


def _diff_rope_tables(pos, dh, rope_dim):
    half = rope_dim // 2
    inv = jnp.float32(ROPE_THETA) ** (-jnp.arange(half, dtype=F32) / half)
    ang = pos.astype(F32)[:, None] * inv[None, :]
    cos, sin = jnp.cos(ang), jnp.sin(ang)
    n = pos.shape[0]
    rest = dh - rope_dim
    coef = jnp.concatenate([cos, cos, jnp.ones((n, rest), F32)], axis=1)
    from_lo = jnp.concatenate([jnp.zeros((n, half), F32), sin, jnp.zeros((n, rest), F32)], axis=1)
    from_hi = jnp.concatenate([-sin, jnp.zeros((n, half + rest), F32)], axis=1)
    return coef, from_lo, from_hi


def kernel(x_prompt, x_sample, state_ret, state_conv, cache_k, cache_v, page_table, c_prompt, c_sample, ret_wq, ret_wk, ret_wv, ret_wg, ret_wo, kv_norm, kv_wmod, kv_bmod, kv_wk, kv_wv, diff_wq, diff_lq1, diff_lk1, diff_lq2, diff_lk2, diff_subln, diff_wo, norm_mix, norm_ffn, w_mod, b_mod, ffn_wup, ffn_conv, ffn_conv_b, ffn_wdown, norm_final):
    bp, lp, d = x_prompt.shape
    bs, ls, _ = x_sample.shape
    assert ls == 1, "decode group carries one new token per sequence"
    depth = w_mod.shape[0]
    n_a = ret_wq.shape[0]
    nh_r, dk_r, dv_r = state_ret.shape[2], state_ret.shape[3], state_ret.shape[4]
    nh_d, dh_d, dv_d = cache_k.shape[2], cache_k.shape[4], cache_v.shape[3]
    rope_dim = dh_d // 4
    d_ff = ffn_conv.shape[2]
    past_len = page_table.shape[1] * cache_k.shape[1]
    mp = bp * lp
    tm = 1024

    pos_p = jnp.arange(lp)
    pos_s = past_len + jnp.arange(ls)
    bf = lambda w: w.astype(BF16)

    xp = x_prompt.reshape(mp, d)
    xs = x_sample.reshape(bs, d)

    n_c = bp + bs
    n_c_pad = -(-n_c // SUBLANES) * SUBLANES
    c_all = jnp.concatenate([c_prompt, c_sample, jnp.zeros((n_c_pad - n_c, d), F32)], axis=0)

    def split_mod(mod, n):
        mod_p = mod[:bp].reshape(bp, 1, n, d)
        mod_s = mod[bp:n_c].reshape(1, bs, n, d)
        return ([mod_p[:, :, i] for i in range(n)], [mod_s[:, :, i] for i in range(n)])

    lin = functools.partial(_linear, tm=tm, rows_per_group=lp)

    ret_p, ret_s, conv_p, conv_s = [], [], [], []
    k_p = v_p = k_s = v_s = None
    for l in range(depth):
        (sm_p, cm_p, gm_p, sf_p, cf_p, gf_p), (sm_s, cm_s, gm_s, sf_s, cf_s, gf_s) = split_mod(
            _modulation(c_all, w_mod, b_mod, l), 6)
        nrm_mix = (norm_mix[l], sm_p, cm_p, sm_s, cm_s)
        if l < n_a:
            k_scale = dk_r ** -0.5
            tabs_p = _ret_rope_tables(pos_p, dk_r)
            tabs_s = _ret_rope_tables(pos_s, dk_r)
            q, q_s, h_p, h_s = lin(xp, xs, ret_wq, l, tn=512, out_dtype=BF16, norm=nrm_mix, epi="rope_ret",
                                   epi_p=tabs_p, epi_s=tabs_s, name="ret_q")
            k, k_s_ = lin(h_p, h_s, ret_wk, l, tn=512, out_dtype=BF16, epi="rope_ret", epi_p=tabs_p,
                          epi_s=tabs_s, epi_scale=k_scale, name="ret_k")
            v, v_s_ = lin(h_p, h_s, ret_wv, l, tn=512, out_dtype=BF16, name="ret_v")
            g, g_s = lin(h_p, h_s, ret_wg, l, tn=512, out_dtype=BF16, name="ret_g")
            o, st_p = _retention_prompt(q, k, v, g, batch=bp, seq=lp, nh=nh_r, dk=dk_r, dv=dv_r,
                                        chunk=RET_CHUNK)
            o_s, st_s = _retention_decode(q_s, k_s_, v_s_, g_s, state_ret[l:l + 1], nh=nh_r, dk=dk_r,
                                          dv=dv_r)
            xp, xs = lin(o, o_s.reshape(bs, nh_r * dv_r), ret_wo, l, tn=512, out_dtype=F32, epi="residual",
                         epi_p=(xp, gm_p), epi_s=(xs, gm_s), name="ret_o")
            ret_p.append(st_p[0])
            ret_s.append(st_s[0])
        else:
            bi = l - n_a
            tabs_p = _diff_rope_tables(pos_p, dh_d, rope_dim)
            tabs_s = _diff_rope_tables(pos_s, dh_d, rope_dim)
            if bi == 0:
                (sh_p, sc_p), (sh_s, sc_s) = split_mod(
                    _modulation(c_all, kv_wmod[None], kv_bmod[None], 0), 2)
                nrm_kv = (kv_norm, sh_p, sc_p, sh_s, sc_s)
                k_p, k_s, h_p, h_s = lin(xp, xs, kv_wk[None], 0, tn=512, out_dtype=F32, norm=nrm_kv,
                                         epi="rope_diff", epi_p=tabs_p, epi_s=tabs_s, name="kv_k")
                v_p, v_s = lin(h_p, h_s, kv_wv[None], 0, tn=512, out_dtype=F32, name="kv_v")
            lam_init = 0.8 - 0.6 * math.exp(-0.3 * l)
            lam = (jnp.exp(jnp.sum(diff_lq1[bi].astype(F32) * diff_lk1[bi].astype(F32)))
                   - jnp.exp(jnp.sum(diff_lq2[bi].astype(F32) * diff_lk2[bi].astype(F32))) + lam_init)
            lam_sc = jnp.stack([lam, jnp.float32(1.0 - lam_init)])
            q_scale = dh_d ** -0.5
            q, q_s, _, _ = lin(xp, xs, diff_wq, bi, tn=512, out_dtype=BF16, norm=nrm_mix, epi="rope_diff",
                               epi_p=tabs_p, epi_s=tabs_s, epi_scale=q_scale, name="diff_q")
            o = _diff_attention_prompt(q, k_p, v_p, lam_sc, diff_subln[bi], batch=bp, seq=lp, nh=nh_d,
                                       dh=dh_d, dv=dv_d)
            o_s = _diff_attention_decode(q_s, k_s, v_s, cache_k, cache_v, page_table, lam_sc,
                                         diff_subln[bi], nh=nh_d, dh=dh_d, dv=dv_d)
            xp, xs = lin(o, o_s.reshape(bs, nh_d * dv_d), diff_wo, bi, tn=512, out_dtype=F32,
                         epi="residual", epi_p=(xp, gm_p), epi_s=(xs, gm_s), name="diff_o")

        wa, wg_ = bf(ffn_wup[l][:, :d_ff]), bf(ffn_wup[l][:, d_ff:])
        cw, cb = ffn_conv[l], ffn_conv_b[l].reshape(1, d_ff)
        buf = jnp.swapaxes(state_conv[l], 0, 1)
        a, g, tails, a_s, u_s = _ffn_up(xp, xs, (norm_ffn[l], sf_p, cf_p, sf_s, cf_s), wa, wg_, buf, cw, cb,
                                        tm=tm, tn=512, rows_per_group=lp, name="ffn_up")
        u = _conv_gate(a, g, cw, cb, batch=bp, seq=lp)
        xp, xs = lin(u, u_s, ffn_wdown, l, tn=256, out_dtype=F32, epi="residual", epi_p=(xp, gf_p),
                     epi_s=(xs, gf_s), name="ffn_down")
        last = tails.reshape(bp, lp // tm, SUBLANES, d_ff)[:, -1, SUBLANES - (CONV_W - 1):, :]
        conv_p.append(last)
        conv_s.append(jnp.concatenate([state_conv[l][:, 1:], a_s[:, None, :]], axis=1))

    y_prompt = _final_norm(xp, norm_final, tm).reshape(bp, lp, d)
    y_sample = _final_norm(xs, norm_final, bs).reshape(bs, ls, d)
    return (y_prompt, y_sample, jnp.stack(ret_p), jnp.stack(ret_s), jnp.stack(conv_p), jnp.stack(conv_s),
            k_p.reshape(bp, lp, nh_d, 2, dh_d), v_p.reshape(bp, lp, nh_d, dv_d),
            k_s.reshape(bs, ls, nh_d, 2, dh_d), v_s.reshape(bs, ls, nh_d, dv_d))
```

```python
import functools
import math

import jax
import jax.numpy as jnp
from jax import lax
from jax.experimental import pallas as pl
from jax.experimental.pallas import tpu as pltpu

F32 = jnp.float32
BF16 = jnp.bfloat16

NORM_EPS = 1e-6
RET_THETA = 10000.0
ROPE_THETA = 500000.0
RET_CHUNK = 256
CONV_W = 3

V7X_VMEM_LIMIT_BYTES = 56 * 1024 * 1024
LANES = 128
SUBLANES = 8


def _params(semantics):
    return pltpu.CompilerParams(dimension_semantics=semantics,
                                vmem_limit_bytes=V7X_VMEM_LIMIT_BYTES)


def _silu(x):
    return x * jax.nn.sigmoid(x)


def _rms(x):
    return x * lax.rsqrt(jnp.mean(x * x, axis=-1, keepdims=True) + NORM_EPS)


def _dot_nt(a, b):
    return lax.dot_general(a, b, (((1,), (1,)), ((), ())), preferred_element_type=F32)


def _dot_tn(a, b):
    return lax.dot_general(a, b, (((0,), (0,)), ((), ())), preferred_element_type=F32)


def _mod_kernel(c_ref, w_ref, b_ref, o_ref, s_ref):
    @pl.when(pl.program_id(0) == 0)
    def _():
        s_ref[...] = _silu(c_ref[...]).astype(BF16)

    acc = jnp.dot(s_ref[...], w_ref[...].astype(BF16), preferred_element_type=F32)
    o_ref[...] = acc + b_ref[...]


def _modulation(c, w, b, layer, tn=1024):
    m, d = c.shape
    n = w.shape[2]
    return pl.pallas_call(
        _mod_kernel,
        out_shape=jax.ShapeDtypeStruct((m, n), F32),
        grid=(n // tn,),
        in_specs=[pl.BlockSpec((m, d), lambda j: (0, 0)),
                  pl.BlockSpec((None, d, tn), lambda j: (layer, 0, j)),
                  pl.BlockSpec((None, 1, tn), lambda j: (layer, 0, j))],
        out_specs=pl.BlockSpec((m, tn), lambda j: (0, j)),
        scratch_shapes=[pltpu.VMEM((m, d), BF16)],
        compiler_params=_params(("arbitrary",)),
        name="modulation",
    )(c, w, b.reshape(b.shape[0], 1, n))


NORM_ROWS = 128


def _ada_norm_to(h_ref, x_ref, gain_ref, shift_ref, scale_ref):
    tm = x_ref.shape[0]
    if shift_ref.shape[1] != 1:
        y = _rms(x_ref[...].astype(F32)) * gain_ref[...]
        h_ref[...] = (y * (1.0 + scale_ref[0]) + shift_ref[0]).astype(BF16)
        return

    def body(r, carry):
        rows = pl.ds(pl.multiple_of(r * NORM_ROWS, NORM_ROWS), NORM_ROWS)
        y = _rms(x_ref[rows, :].astype(F32)) * gain_ref[...]
        h_ref[rows, :] = (y * (1.0 + scale_ref[0]) + shift_ref[0]).astype(BF16)
        return carry

    lax.fori_loop(0, tm // NORM_ROWS, body, 0)


def _epilogue(o_ref, acc, epi, epi_scale, extra):
    tn = acc.shape[1]
    if epi is None:
        o_ref[...] = acc.astype(o_ref.dtype)
    elif epi == "rope_ret":
        cos, sin = extra[0][...], extra[1][...]
        for c in range(tn // (2 * LANES)):
            lo, mid, hi = 2 * c * LANES, (2 * c + 1) * LANES, (2 * c + 2) * LANES
            x1, x2 = acc[:, lo:mid], acc[:, mid:hi]
            o_ref[:, lo:mid] = ((x1 * cos - x2 * sin) * epi_scale).astype(o_ref.dtype)
            o_ref[:, mid:hi] = ((x2 * cos + x1 * sin) * epi_scale).astype(o_ref.dtype)
    elif epi == "rope_diff":
        cc, sa, sb = extra[0][...], extra[1][...], extra[2][...]
        for c in range(tn // LANES):
            xc = acc[:, c * LANES:(c + 1) * LANES]
            r = (xc * cc + pltpu.roll(xc, 16, 1) * sa
                 + pltpu.roll(xc, LANES - 16, 1) * sb)
            o_ref[:, c * LANES:(c + 1) * LANES] = (r * epi_scale).astype(o_ref.dtype)
    elif epi == "residual":
        res_ref, gate_ref = extra
        o_ref[...] = (res_ref[...] + gate_ref[0] * acc).astype(o_ref.dtype)


def _take(it, n):
    return [next(it) for _ in range(n)]


def _linear_kernel(*refs, has_norm, epi, epi_scale):
    n_extra = {None: 0, "rope_ret": 2, "rope_diff": 3, "residual": 2}[epi]
    it = iter(refs)
    xp_ref, xs_ref = next(it), next(it)
    if has_norm:
        gain_ref, shp_ref, scp_ref, shs_ref, scs_ref = _take(it, 5)
    w_ref = next(it)
    extra_p, extra_s = _take(it, n_extra), _take(it, n_extra)
    op_ref, os_ref = next(it), next(it)
    i, j = pl.program_id(0), pl.program_id(1)

    if has_norm:
        hp_ref, hs_ref = next(it), next(it)

        @pl.when(j == 0)
        def _():
            _ada_norm_to(hp_ref, xp_ref, gain_ref, shp_ref, scp_ref)

        @pl.when((i == 0) & (j == 0))
        def _():
            _ada_norm_to(hs_ref, xs_ref, gain_ref, shs_ref, scs_ref)

        a = hp_ref[...]
    else:
        a = xp_ref[...].astype(BF16)

    w = w_ref[...].astype(BF16)
    _epilogue(op_ref, jnp.dot(a, w, preferred_element_type=F32), epi, epi_scale, extra_p)

    @pl.when(i == 0)
    def _():
        a_s = hs_ref[...] if has_norm else xs_ref[...].astype(BF16)
        _epilogue(os_ref.at[0], jnp.dot(a_s, w, preferred_element_type=F32), epi, epi_scale,
                  extra_s)

    @pl.when(i != 0)
    def _():
        os_ref[...] = jnp.zeros_like(os_ref)


def _group_spec(arr, tm, rows_per_group, width, col_fn):
    g, r, _ = arr.shape
    if r == 1:
        return pl.BlockSpec((1, 1, width),
                            lambda i, j: ((i * tm) // rows_per_group, 0, col_fn(j)))
    return pl.BlockSpec((1, r, width), lambda i, j: (0, 0, col_fn(j)))


def _table_spec(tab, tm, rows_per_group):
    if tab.shape[0] == 1:
        return pl.BlockSpec((1, LANES), lambda i, j: (0, 0))
    nblk = rows_per_group // tm
    return pl.BlockSpec((tm, LANES), lambda i, j: (i % nblk, 0))


def _decode_out_spec(ms, tn):
    return pl.BlockSpec((1, ms, tn), lambda i, j: (i, 0, j))


def _linear(xp, xs, w, layer, *, tm, tn, out_dtype, rows_per_group, norm=None, epi=None,
            epi_p=(), epi_s=(), epi_scale=1.0, name="linear"):
    m, k = xp.shape
    ms = xs.shape[0]
    n = w.shape[2]
    has_norm = norm is not None
    args = [xp, xs]
    specs = [pl.BlockSpec((tm, k), lambda i, j: (i, 0)),
             pl.BlockSpec((ms, k), lambda i, j: (0, 0))]
    if has_norm:
        gain, shift_p, scale_p, shift_s, scale_s = norm
        args += [gain.reshape(1, k), shift_p, scale_p, shift_s, scale_s]
        specs += [pl.BlockSpec((1, k), lambda i, j: (0, 0)),
                  _group_spec(shift_p, tm, rows_per_group, k, lambda j: 0),
                  _group_spec(scale_p, tm, rows_per_group, k, lambda j: 0),
                  _group_spec(shift_s, tm, rows_per_group, k, lambda j: 0),
                  _group_spec(scale_s, tm, rows_per_group, k, lambda j: 0)]
    args.append(w)
    specs.append(pl.BlockSpec((None, k, tn), lambda i, j: (layer, 0, j)))
    if epi in ("rope_ret", "rope_diff"):
        for tab in tuple(epi_p) + tuple(epi_s):
            args.append(tab)
            specs.append(_table_spec(tab, tm, rows_per_group))
    elif epi == "residual":
        (res_p, gate_p), (res_s, gate_s) = epi_p, epi_s
        args += [res_p, gate_p, res_s, gate_s]
        specs += [pl.BlockSpec((tm, tn), lambda i, j: (i, j)),
                  _group_spec(gate_p, tm, rows_per_group, tn, lambda j: j),
                  pl.BlockSpec((ms, tn), lambda i, j: (0, j)),
                  _group_spec(gate_s, tm, rows_per_group, tn, lambda j: j)]
    out_shape = [jax.ShapeDtypeStruct((m, n), out_dtype),
                 jax.ShapeDtypeStruct((m // tm, ms, n), F32)]
    out_specs = [pl.BlockSpec((tm, tn), lambda i, j: (i, j)), _decode_out_spec(ms, tn)]
    if has_norm:
        out_shape += [jax.ShapeDtypeStruct((m, k), BF16), jax.ShapeDtypeStruct((ms, k), BF16)]
        out_specs += [pl.BlockSpec((tm, k), lambda i, j: (i, 0)),
                      pl.BlockSpec((ms, k), lambda i, j: (0, 0))]
    kern = functools.partial(_linear_kernel, has_norm=has_norm, epi=epi, epi_scale=epi_scale)
    outs = pl.pallas_call(
        kern,
        out_shape=out_shape,
        grid=(m // tm, n // tn),
        in_specs=specs,
        out_specs=out_specs,
        compiler_params=_params(("arbitrary", "arbitrary")),
        name=name,
    )(*args)
    if has_norm:
        return outs[0], outs[1][0], outs[2], outs[3]
    return outs[0], outs[1][0]


def _ffn_up_kernel(xp_ref, xs_ref, gain_ref, shp_ref, scp_ref, shs_ref, scs_ref, wa_ref, wg_ref,
                   buf_ref, cw_ref, cb_ref, u_ref, tail_ref, as_ref, us_ref,
                   hp_ref, hs_ref, carry_ref, *, tiles_per_seq):
    i, j = pl.program_id(0), pl.program_id(1)

    @pl.when(j == 0)
    def _():
        _ada_norm_to(hp_ref, xp_ref, gain_ref, shp_ref, scp_ref)

    @pl.when((i == 0) & (j == 0))
    def _():
        _ada_norm_to(hs_ref, xs_ref, gain_ref, shs_ref, scs_ref)

    @pl.when(i % tiles_per_seq == 0)
    def _():
        carry_ref[j] = jnp.zeros(carry_ref.shape[1:], F32)

    wa, wg = wa_ref[...], wg_ref[...]
    h = hp_ref[...]
    a = jnp.dot(h, wa, preferred_element_type=F32)
    g = jnp.dot(h, wg, preferred_element_type=F32)
    tm, tn = a.shape
    halo = carry_ref[j]
    last = a[tm - SUBLANES:, :]
    carry_ref[j] = last
    tail_ref[0] = last
    row = lax.broadcasted_iota(jnp.int32, (tm, LANES), 0)
    for c in range(tn // LANES):
        cols = slice(c * LANES, (c + 1) * LANES)
        ac = a[:, cols]
        prev1 = halo[SUBLANES - 1:SUBLANES, cols]
        prev2 = halo[SUBLANES - 2:SUBLANES - 1, cols]
        a_m1 = jnp.where(row == 0, prev1, pltpu.roll(ac, 1, 0))
        a_m2 = jnp.where(row == 0, prev2, jnp.where(row == 1, prev1, pltpu.roll(ac, 2, 0)))
        conv = (cb_ref[:, cols] + a_m2 * cw_ref[0:1, cols] + a_m1 * cw_ref[1:2, cols]
                + ac * cw_ref[2:3, cols])
        u_ref[:, cols] = (_silu(conv) * g[:, cols]).astype(BF16)

    @pl.when(i == 0)
    def _():
        hs = hs_ref[...]
        a_s = jnp.dot(hs, wa, preferred_element_type=F32)
        g_s = jnp.dot(hs, wg, preferred_element_type=F32)
        conv = (cb_ref[...] + buf_ref[0] * cw_ref[0:1, :] + buf_ref[1] * cw_ref[1:2, :]
                + a_s * cw_ref[2:3, :])
        as_ref[0] = a_s
        us_ref[0] = (_silu(conv) * g_s).astype(BF16)

    @pl.when(i != 0)
    def _():
        as_ref[...] = jnp.zeros_like(as_ref)
        us_ref[...] = jnp.zeros_like(us_ref)


def _ffn_up(xp, xs, norm, wa, wg, buf, cw, cb, *, tm, tn, rows_per_group, name="ffn_up"):
    m, k = xp.shape
    ms = xs.shape[0]
    f = wa.shape[1]
    gain, shift_p, scale_p, shift_s, scale_s = norm
    args = [xp, xs, gain.reshape(1, k), shift_p, scale_p, shift_s, scale_s, wa, wg, buf, cw, cb]
    specs = [pl.BlockSpec((tm, k), lambda i, j: (i, 0)),
             pl.BlockSpec((ms, k), lambda i, j: (0, 0)),
             pl.BlockSpec((1, k), lambda i, j: (0, 0)),
             _group_spec(shift_p, tm, rows_per_group, k, lambda j: 0),
             _group_spec(scale_p, tm, rows_per_group, k, lambda j: 0),
             _group_spec(shift_s, tm, rows_per_group, k, lambda j: 0),
             _group_spec(scale_s, tm, rows_per_group, k, lambda j: 0),
             pl.BlockSpec((k, tn), lambda i, j: (0, j)),
             pl.BlockSpec((k, tn), lambda i, j: (0, j)),
             pl.BlockSpec((CONV_W - 1, ms, tn), lambda i, j: (0, 0, j)),
             pl.BlockSpec((CONV_W, tn), lambda i, j: (0, j)),
             pl.BlockSpec((1, tn), lambda i, j: (0, j))]
    n_col = pl.cdiv(f, tn)
    out_shape = [jax.ShapeDtypeStruct((m, f), BF16),
                 jax.ShapeDtypeStruct((m // tm, SUBLANES, f), F32),
                 jax.ShapeDtypeStruct((m // tm, ms, f), F32),
                 jax.ShapeDtypeStruct((m // tm, ms, f), BF16)]
    out_specs = [pl.BlockSpec((tm, tn), lambda i, j: (i, j)),
                 pl.BlockSpec((1, SUBLANES, tn), lambda i, j: (i, 0, j)),
                 _decode_out_spec(ms, tn), _decode_out_spec(ms, tn)]
    u, tails, a_s, u_s = pl.pallas_call(
        functools.partial(_ffn_up_kernel, tiles_per_seq=rows_per_group // tm),
        out_shape=out_shape,
        grid=(m // tm, n_col),
        in_specs=specs,
        out_specs=out_specs,
        scratch_shapes=[pltpu.VMEM((tm, k), BF16), pltpu.VMEM((ms, k), BF16),
                        pltpu.VMEM((n_col, SUBLANES, tn), F32)],
        compiler_params=_params(("arbitrary", "arbitrary")),
        name=name,
    )(*args)
    return u, tails, a_s[0], u_s[0]


def _ret_prompt_kernel(sdec_ref, q_ref, k_ref, v_ref, g_ref, decay_ref, qdec_ref,
                       kdec_ref, o_ref, st_ref, *, chunk):
    sdec = sdec_ref[pl.program_id(1)]
    decay, qdec, kdec = decay_ref[0], qdec_ref[0], kdec_ref[0]
    st_ref[...] = jnp.zeros_like(st_ref)
    for n in range(q_ref.shape[0] // chunk):
        rows = slice(n * chunk, (n + 1) * chunk)
        s = st_ref[0, 0, 0]
        q, k, v = q_ref[rows, :], k_ref[rows, :], v_ref[rows, :]
        att = _dot_nt(q, k) * decay
        o = (jnp.dot(att.astype(BF16), v, preferred_element_type=F32)
             + jnp.dot(q, s.astype(BF16), preferred_element_type=F32) * qdec)
        kd = (k.astype(F32) * kdec).astype(BF16)
        st_ref[0, 0, 0] = s * sdec + _dot_tn(kd, v)
        o_ref[rows, :] = (_rms(o) * _silu(g_ref[rows, :].astype(F32))).astype(BF16)


def _ret_tables(chunk, nh):
    log_g = jnp.log1p(-jnp.exp2(-5.0 - jnp.arange(nh, dtype=F32)))
    idx = jnp.arange(chunk, dtype=F32)
    rel = idx[:, None] - idx[None, :]
    decay = jnp.where(rel >= 0, jnp.exp(jnp.maximum(rel, 0.0)[None] * log_g[:, None, None]), 0.0)
    q_dec = jnp.exp((idx[None, :] + 1.0) * log_g[:, None])[:, :, None]
    k_dec = jnp.exp((chunk - 1.0 - idx)[None, :] * log_g[:, None])[:, :, None]
    s_dec = jnp.exp(chunk * log_g)
    return decay, q_dec, k_dec, s_dec


def _retention_prompt(q, k, v, g, *, batch, seq, nh, dk, dv, chunk):
    m = q.shape[0]
    decay, q_dec, k_dec, s_dec = _ret_tables(chunk, nh)
    seq_head = lambda b, h: (b, h)
    head = lambda b, h: (h, 0, 0)
    return pl.pallas_call(
        functools.partial(_ret_prompt_kernel, chunk=chunk),
        out_shape=[jax.ShapeDtypeStruct((m, nh * dv), BF16),
                   jax.ShapeDtypeStruct((1, batch, nh, dk, dv), F32)],
        grid=(batch, nh),
        in_specs=[pl.BlockSpec(memory_space=pltpu.SMEM),
                  pl.BlockSpec((seq, dk), seq_head),
                  pl.BlockSpec((seq, dk), seq_head),
                  pl.BlockSpec((seq, dv), seq_head),
                  pl.BlockSpec((seq, dv), seq_head),
                  pl.BlockSpec((1, chunk, chunk), head),
                  pl.BlockSpec((1, chunk, 1), head),
                  pl.BlockSpec((1, chunk, 1), head)],
        out_specs=[pl.BlockSpec((seq, dv), seq_head),
                   pl.BlockSpec((1, 1, 1, dk, dv), lambda b, h: (0, b, h, 0, 0))],
        compiler_params=_params(("parallel", "parallel")),
        name="retention_prompt",
    )(s_dec, q, k, v, g, decay, q_dec, k_dec)


def _ret_decode_kernel(q_ref, k_ref, v_ref, g_ref, st_ref, dec_ref, o_ref, sto_ref, *, nh):
    q, k, v = q_ref[0], k_ref[0], v_ref[0]
    decay, qdec, kdec, sdec = (dec_ref[:, i:i + 1] for i in range(4))
    att = jnp.sum(q * k, axis=-1, keepdims=True) * decay
    rows = lax.broadcasted_iota(jnp.int32, (nh, 1), 0)
    qb, vb = q.astype(BF16), v.astype(BF16)
    kd = k * kdec
    qs = jnp.zeros(v.shape, F32)
    for h in range(nh):
        s = st_ref[0, 0, h]
        qs = jnp.where(rows == h, jnp.dot(qb, s.astype(BF16), preferred_element_type=F32), qs)
        k_h = jnp.where(rows == h, kd, 0.0).astype(BF16)
        sto_ref[0, 0, h] = s * sdec[h:h + 1, :] + _dot_tn(k_h, vb)
    o = att * v + qs * qdec
    o_ref[0] = _rms(o) * _silu(g_ref[0])


def _retention_decode(q, k, v, g, state, *, nh, dk, dv):
    b = q.shape[0]
    decay, q_dec, k_dec, s_dec = _ret_tables(1, nh)
    dec = jnp.stack([decay[:, 0, 0], q_dec[:, 0, 0], k_dec[:, 0, 0], s_dec], axis=1)
    tok = lambda w: pl.BlockSpec((1, nh, w), lambda i: (i, 0, 0))
    st = pl.BlockSpec((1, 1, nh, dk, dv), lambda i: (0, i, 0, 0, 0))
    return pl.pallas_call(
        functools.partial(_ret_decode_kernel, nh=nh),
        out_shape=[jax.ShapeDtypeStruct((b, nh, dv), F32),
                   jax.ShapeDtypeStruct(state.shape, F32)],
        grid=(b,),
        in_specs=[tok(dk), tok(dk), tok(dv), tok(dv), st,
                  pl.BlockSpec((nh, 4), lambda i: (0, 0))],
        out_specs=[tok(dv), st],
        compiler_params=_params(("parallel",)),
        name="retention_decode",
    )(q.reshape(b, nh, dk), k.reshape(b, nh, dk), v.reshape(b, nh, dv),
      g.reshape(b, nh, dv), state, dec)


def _diff_out(o, subln, out_scale):
    return _rms(o) * subln * out_scale


def _lane_tile_reduce(x, op):
    out = x[:, :LANES]
    for j in range(1, x.shape[1] // LANES):
        out = op(out, x[:, j * LANES:(j + 1) * LANES])
    return out


def _diff_prompt_kernel(lam_ref, q_ref, k_ref, v_ref, subln_ref, o_ref,
                        kb_ref, vb_ref, s_ref, p_ref, *, tq, dh):
    kb_ref[...] = k_ref[...].astype(BF16)
    vb_ref[...] = v_ref[...].astype(BF16)
    row = lax.broadcasted_iota(jnp.int32, (tq, tq), 0)
    col = lax.broadcasted_iota(jnp.int32, (tq, tq), 1)
    for qi in range(q_ref.shape[0] // tq):
        rows = slice(qi * tq, (qi + 1) * tq)
        kv_len = (qi + 1) * tq
        outs = []
        for c in range(2):
            comp = slice(c * dh, (c + 1) * dh)
            q = q_ref[rows, comp]
            m_part = None
            for ki in range(qi + 1):
                keys = slice(ki * tq, (ki + 1) * tq)
                s = _dot_nt(q, kb_ref[keys, comp])
                if ki == qi:
                    s = jnp.where(col <= row, s, -jnp.inf)
                s_ref[c, :, keys] = s
                blk_max = _lane_tile_reduce(s, jnp.maximum)
                m_part = blk_max if m_part is None else jnp.maximum(m_part, blk_max)
            m = jnp.max(m_part, axis=-1, keepdims=True)
            l_part = jnp.zeros((tq, LANES), F32)
            for ki in range(qi + 1):
                keys = slice(ki * tq, (ki + 1) * tq)
                p = jnp.exp(s_ref[c, :, keys] - m)
                l_part = l_part + _lane_tile_reduce(p, jnp.add)
                p_ref[c, :, keys] = p.astype(BF16)
            l = jnp.sum(l_part, axis=-1, keepdims=True)
            pv = jnp.dot(p_ref[c, :, :kv_len], vb_ref[:kv_len, :], preferred_element_type=F32)
            outs.append(pv / l)
        o = outs[0] - lam_ref[0] * outs[1]
        o_ref[rows, :] = _diff_out(o, subln_ref[...], lam_ref[1]).astype(BF16)


def _diff_attention_prompt(q, k, v, lam, subln, *, batch, seq, nh, dh, dv, tq=256):
    m = q.shape[0]
    seq_head = lambda b, h: (b, h)
    return pl.pallas_call(
        functools.partial(_diff_prompt_kernel, tq=tq, dh=dh),
        out_shape=jax.ShapeDtypeStruct((m, nh * dv), BF16),
        grid=(batch, nh),
        in_specs=[pl.BlockSpec(memory_space=pltpu.SMEM),
                  pl.BlockSpec((seq, 2 * dh), seq_head),
                  pl.BlockSpec((seq, 2 * dh), seq_head),
                  pl.BlockSpec((seq, dv), seq_head),
                  pl.BlockSpec((1, dv), lambda b, h: (0, 0))],
        out_specs=pl.BlockSpec((seq, dv), seq_head),
        scratch_shapes=[pltpu.VMEM((seq, 2 * dh), BF16), pltpu.VMEM((seq, dv), BF16),
                        pltpu.VMEM((2, tq, seq), F32), pltpu.VMEM((2, tq, seq), BF16)],
        compiler_params=_params(("parallel", "parallel")),
        name="diff_attention_prompt",
    )(lam, q, k, v, subln.reshape(1, dv))


PAGES_PER_STEP = 8


def _diff_decode_kernel(pt_ref, lam_ref, q_ref, kn_ref, vn_ref, subln_ref, *rest, nh, page):
    del pt_ref
    ck_refs, cv_refs = rest[:PAGES_PER_STEP], rest[PAGES_PER_STEP:2 * PAGES_PER_STEP]
    o_ref, m_ref, l_ref, acc_ref = rest[2 * PAGES_PER_STEP:]
    p = pl.program_id(1)
    rows_kv = page * nh

    @pl.when(p == 0)
    def _():
        m_ref[...] = jnp.full_like(m_ref, -jnp.inf)
        l_ref[...] = jnp.zeros_like(l_ref)
        acc_ref[...] = jnp.zeros_like(acc_ref)

    q = q_ref[0]
    qb = q.astype(BF16)
    row = lax.broadcasted_iota(jnp.int32, (2 * nh, rows_kv), 0)
    col = lax.broadcasted_iota(jnp.int32, (2 * nh, rows_kv), 1)
    own_head = (col % nh) == (row % nh)
    scores = []
    for ck_ref in ck_refs:
        k0 = ck_ref[0, pl.ds(0, rows_kv, stride=2), :].astype(BF16)
        k1 = ck_ref[0, pl.ds(1, rows_kv, stride=2), :].astype(BF16)
        s = jnp.concatenate([_dot_nt(qb[:nh], k0), _dot_nt(qb[nh:], k1)], axis=0)
        scores.append(jnp.where(own_head, s, -jnp.inf))
    m_old = m_ref[...]
    m_new = jnp.maximum(m_old, jnp.max(functools.reduce(jnp.maximum, scores), axis=-1, keepdims=True))
    alpha = jnp.exp(m_old - m_new)
    l = alpha * l_ref[...]
    acc = alpha * acc_ref[...]
    for s, cv_ref in zip(scores, cv_refs):
        pr = jnp.exp(s - m_new)
        l = l + jnp.sum(pr, axis=-1, keepdims=True)
        acc = acc + jnp.dot(pr.astype(BF16), cv_ref[0].astype(BF16), preferred_element_type=F32)
    m_ref[...] = m_new
    l_ref[...] = l
    acc_ref[...] = acc

    @pl.when(p == pl.num_programs(1) - 1)
    def _():
        s_new = jnp.sum(q * kn_ref[0], axis=-1, keepdims=True)
        m_old = m_ref[...]
        m_new = jnp.maximum(m_old, s_new)
        alpha = jnp.exp(m_old - m_new)
        pn = jnp.exp(s_new - m_new)
        l = alpha * l_ref[...] + pn
        vn = vn_ref[0]
        acc = alpha * acc_ref[...] + pn * jnp.concatenate([vn, vn], axis=0)
        on = acc / l
        o = on[:nh] - lam_ref[0] * on[nh:]
        o_ref[0] = _diff_out(o, subln_ref[...], lam_ref[1])


def _diff_attention_decode(q, k_new, v_new, cache_k, cache_v, page_table, lam, subln,
                           *, nh, dh, dv):
    b = q.shape[0]
    n_pool, page = cache_k.shape[0], cache_k.shape[1]
    n_pages = page_table.shape[1]
    ck = cache_k.reshape(n_pool, page * nh * 2, dh)
    cv = cache_v.reshape(n_pool, page * nh, dv)
    comp_major = lambda t: t.reshape(b, nh, 2, dh).transpose(0, 2, 1, 3).reshape(b, 2 * nh, dh)
    tok = lambda r, w: pl.BlockSpec((1, r, w), lambda i, p, pt: (i, 0, 0))

    def page_spec(rows, width, r):
        return pl.BlockSpec((1, rows, width),
                            lambda i, p, pt: (pt[i * n_pages + p * PAGES_PER_STEP + r], 0, 0))

    steps = n_pages // PAGES_PER_STEP
    grid_spec = pltpu.PrefetchScalarGridSpec(
        num_scalar_prefetch=1,
        grid=(b, steps),
        in_specs=([pl.BlockSpec(memory_space=pltpu.SMEM),
                   tok(2 * nh, dh), tok(2 * nh, dh), tok(nh, dv),
                   pl.BlockSpec((1, dv), lambda i, p, pt: (0, 0))]
                  + [page_spec(page * nh * 2, dh, r) for r in range(PAGES_PER_STEP)]
                  + [page_spec(page * nh, dv, r) for r in range(PAGES_PER_STEP)]),
        out_specs=tok(nh, dv),
        scratch_shapes=[pltpu.VMEM((2 * nh, 1), F32), pltpu.VMEM((2 * nh, 1), F32),
                        pltpu.VMEM((2 * nh, dv), F32)],
    )
    return pl.pallas_call(
        functools.partial(_diff_decode_kernel, nh=nh, page=page),
        out_shape=jax.ShapeDtypeStruct((b, nh, dv), F32),
        grid_spec=grid_spec,
        compiler_params=_params(("parallel", "arbitrary")),
        name="diff_attention_decode",
    )(page_table.reshape(-1), lam, comp_major(q), comp_major(k_new),
      v_new.reshape(b, nh, dv), subln.reshape(1, dv),
      *([ck] * PAGES_PER_STEP), *([cv] * PAGES_PER_STEP))


def _final_norm_kernel(x_ref, gain_ref, o_ref):
    o_ref[...] = _rms(x_ref[...]) * gain_ref[...]


def _final_norm(x, gain, tm):
    m, d = x.shape
    return pl.pallas_call(
        _final_norm_kernel,
        out_shape=jax.ShapeDtypeStruct((m, d), F32),
        grid=(m // tm,),
        in_specs=[pl.BlockSpec((tm, d), lambda i: (i, 0)),
                  pl.BlockSpec((1, d), lambda i: (0, 0))],
        out_specs=pl.BlockSpec((tm, d), lambda i: (i, 0)),
        compiler_params=_params(("parallel",)),
        name="final_norm",
    )(x, gain.reshape(1, d))


def _ret_rope_tables(pos, dk):
    half = dk // 2
    inv = jnp.float32(RET_THETA) ** (-jnp.arange(half, dtype=F32) / half)
    ang = pos.astype(F32)[:, None] * inv[None, :]
    return jnp.cos(ang), jnp.sin(ang)


def _diff_rope_tables(pos, dh, rope_dim):
    half = rope_dim // 2
    inv = jnp.float32(ROPE_THETA) ** (-jnp.arange(half, dtype=F32) / half)
    ang = pos.astype(F32)[:, None] * inv[None, :]
    cos, sin = jnp.cos(ang), jnp.sin(ang)
    n = pos.shape[0]
    rest = dh - rope_dim
    coef = jnp.concatenate([cos, cos, jnp.ones((n, rest), F32)], axis=1)
    from_lo = jnp.concatenate([jnp.zeros((n, half), F32), sin, jnp.zeros((n, rest), F32)], axis=1)
    from_hi = jnp.concatenate([-sin, jnp.zeros((n, half + rest), F32)], axis=1)
    return coef, from_lo, from_hi


def kernel(x_prompt, x_sample, state_ret, state_conv, cache_k, cache_v, page_table, c_prompt, c_sample, ret_wq, ret_wk, ret_wv, ret_wg, ret_wo, kv_norm, kv_wmod, kv_bmod, kv_wk, kv_wv, diff_wq, diff_lq1, diff_lk1, diff_lq2, diff_lk2, diff_subln, diff_wo, norm_mix, norm_ffn, w_mod, b_mod, ffn_wup, ffn_conv, ffn_conv_b, ffn_wdown, norm_final):
    bp, lp, d = x_prompt.shape
    bs, ls, _ = x_sample.shape
    assert ls == 1, "decode group carries one new token per sequence"
    depth = w_mod.shape[0]
    n_a = ret_wq.shape[0]
    nh_r, dk_r, dv_r = state_ret.shape[2], state_ret.shape[3], state_ret.shape[4]
    nh_d, dh_d, dv_d = cache_k.shape[2], cache_k.shape[4], cache_v.shape[3]
    rope_dim = dh_d // 4
    d_ff = ffn_conv.shape[2]
    past_len = page_table.shape[1] * cache_k.shape[1]
    mp = bp * lp
    tm = 1024

    pos_p = jnp.arange(lp)
    pos_s = past_len + jnp.arange(ls)
    bf = lambda w: w.astype(BF16)

    xp = x_prompt.reshape(mp, d)
    xs = x_sample.reshape(bs, d)

    n_c = bp + bs
    n_c_pad = -(-n_c // SUBLANES) * SUBLANES
    c_all = jnp.concatenate([c_prompt, c_sample, jnp.zeros((n_c_pad - n_c, d), F32)], axis=0)

    def split_mod(mod, n):
        mod_p = mod[:bp].reshape(bp, 1, n, d)
        mod_s = mod[bp:n_c].reshape(1, bs, n, d)
        return ([mod_p[:, :, i] for i in range(n)], [mod_s[:, :, i] for i in range(n)])

    lin = functools.partial(_linear, tm=tm, rows_per_group=lp)

    ret_p, ret_s, conv_p, conv_s = [], [], [], []
    k_p = v_p = k_s = v_s = None
    for l in range(depth):
        (sm_p, cm_p, gm_p, sf_p, cf_p, gf_p), (sm_s, cm_s, gm_s, sf_s, cf_s, gf_s) = split_mod(
            _modulation(c_all, w_mod, b_mod, l), 6)
        nrm_mix = (norm_mix[l], sm_p, cm_p, sm_s, cm_s)
        if l < n_a:
            k_scale = dk_r ** -0.5
            tabs_p = _ret_rope_tables(pos_p, dk_r)
            tabs_s = _ret_rope_tables(pos_s, dk_r)
            q, q_s, h_p, h_s = lin(xp, xs, ret_wq, l, tn=512, out_dtype=BF16, norm=nrm_mix, epi="rope_ret",
                                   epi_p=tabs_p, epi_s=tabs_s, name="ret_q")
            k, k_s_ = lin(h_p, h_s, ret_wk, l, tn=1024, out_dtype=BF16, epi="rope_ret", epi_p=tabs_p,
                          epi_s=tabs_s, epi_scale=k_scale, name="ret_k")
            v, v_s_ = lin(h_p, h_s, ret_wv, l, tn=1024, out_dtype=BF16, name="ret_v")
            g, g_s = lin(h_p, h_s, ret_wg, l, tn=1024, out_dtype=BF16, name="ret_g")
            o, st_p = _retention_prompt(q, k, v, g, batch=bp, seq=lp, nh=nh_r, dk=dk_r, dv=dv_r,
                                        chunk=RET_CHUNK)
            o_s, st_s = _retention_decode(q_s, k_s_, v_s_, g_s, state_ret[l:l + 1], nh=nh_r, dk=dk_r,
                                          dv=dv_r)
            xp, xs = lin(o, o_s.reshape(bs, nh_r * dv_r), ret_wo, l, tn=512, out_dtype=F32, epi="residual",
                         epi_p=(xp, gm_p), epi_s=(xs, gm_s), name="ret_o")
            ret_p.append(st_p[0])
            ret_s.append(st_s[0])
        else:
            bi = l - n_a
            tabs_p = _diff_rope_tables(pos_p, dh_d, rope_dim)
            tabs_s = _diff_rope_tables(pos_s, dh_d, rope_dim)
            if bi == 0:
                (sh_p, sc_p), (sh_s, sc_s) = split_mod(
                    _modulation(c_all, kv_wmod[None], kv_bmod[None], 0), 2)
                nrm_kv = (kv_norm, sh_p, sc_p, sh_s, sc_s)
                k_p, k_s, h_p, h_s = lin(xp, xs, kv_wk[None], 0, tn=512, out_dtype=F32, norm=nrm_kv,
                                         epi="rope_diff", epi_p=tabs_p, epi_s=tabs_s, name="kv_k")
                v_p, v_s = lin(h_p, h_s, kv_wv[None], 0, tn=1024, out_dtype=F32, name="kv_v")
            lam_init = 0.8 - 0.6 * math.exp(-0.3 * l)
            lam = (jnp.exp(jnp.sum(diff_lq1[bi].astype(F32) * diff_lk1[bi].astype(F32)))
                   - jnp.exp(jnp.sum(diff_lq2[bi].astype(F32) * diff_lk2[bi].astype(F32))) + lam_init)
            lam_sc = jnp.stack([lam, jnp.float32(1.0 - lam_init)])
            q_scale = dh_d ** -0.5
            q, q_s, _, _ = lin(xp, xs, diff_wq, bi, tn=512, out_dtype=BF16, norm=nrm_mix, epi="rope_diff",
                               epi_p=tabs_p, epi_s=tabs_s, epi_scale=q_scale, name="diff_q")
            o = _diff_attention_prompt(q, k_p, v_p, lam_sc, diff_subln[bi], batch=bp, seq=lp, nh=nh_d,
                                       dh=dh_d, dv=dv_d)
            o_s = _diff_attention_decode(q_s, k_s, v_s, cache_k, cache_v, page_table, lam_sc,
                                         diff_subln[bi], nh=nh_d, dh=dh_d, dv=dv_d)
            xp, xs = lin(o, o_s.reshape(bs, nh_d * dv_d), diff_wo, bi, tn=512, out_dtype=F32,
                         epi="residual", epi_p=(xp, gm_p), epi_s=(xs, gm_s), name="diff_o")

        wa, wg_ = bf(ffn_wup[l][:, :d_ff]), bf(ffn_wup[l][:, d_ff:])
        cw, cb = ffn_conv[l], ffn_conv_b[l].reshape(1, d_ff)
        buf = jnp.swapaxes(state_conv[l], 0, 1)
        u, tails, a_s, u_s = _ffn_up(xp, xs, (norm_ffn[l], sf_p, cf_p, sf_s, cf_s), wa, wg_, buf, cw, cb,
                                     tm=tm, tn=512, rows_per_group=lp, name="ffn_up")
        xp, xs = lin(u, u_s, bf(ffn_wdown[l])[None], 0, tn=512, out_dtype=F32, epi="residual",
                     epi_p=(xp, gf_p), epi_s=(xs, gf_s), name="ffn_down")
        last = tails.reshape(bp, lp // tm, SUBLANES, d_ff)[:, -1, SUBLANES - (CONV_W - 1):, :]
        conv_p.append(last)
        conv_s.append(jnp.concatenate([state_conv[l][:, 1:], a_s[:, None, :]], axis=1))

    y_prompt = _final_norm(xp, norm_final, tm).reshape(bp, lp, d)
    y_sample = _final_norm(xs, norm_final, bs).reshape(bs, ls, d)
    return (y_prompt, y_sample, jnp.stack(ret_p), jnp.stack(ret_s), jnp.stack(conv_p), jnp.stack(conv_s),
            k_p.reshape(bp, lp, nh_d, 2, dh_d), v_p.reshape(bp, lp, nh_d, dv_d),
            k_s.reshape(bs, ls, nh_d, 2, dh_d), v_s.reshape(bs, ls, nh_d, dv_d))
```

```python
import functools
import math

import jax
import jax.numpy as jnp
from jax import lax
from jax.experimental import pallas as pl
from jax.experimental.pallas import tpu as pltpu

F32 = jnp.float32
BF16 = jnp.bfloat16

NORM_EPS = 1e-6
RET_THETA = 10000.0
ROPE_THETA = 500000.0
RET_CHUNK = 256
CONV_W = 3

V7X_VMEM_LIMIT_BYTES = 56 * 1024 * 1024
LANES = 128
SUBLANES = 8


def _params(semantics):
    return pltpu.CompilerParams(dimension_semantics=semantics,
                                vmem_limit_bytes=V7X_VMEM_LIMIT_BYTES)


def _silu(x):
    return x * jax.nn.sigmoid(x)


def _rms(x):
    return x * lax.rsqrt(jnp.mean(x * x, axis=-1, keepdims=True) + NORM_EPS)


def _dot_nt(a, b):
    return lax.dot_general(a, b, (((1,), (1,)), ((), ())), preferred_element_type=F32)


def _dot_tn(a, b):
    return lax.dot_general(a, b, (((0,), (0,)), ((), ())), preferred_element_type=F32)


def _mod_kernel(c_ref, w_ref, b_ref, o_ref, s_ref):
    @pl.when(pl.program_id(0) == 0)
    def _():
        s_ref[...] = _silu(c_ref[...]).astype(BF16)

    acc = jnp.dot(s_ref[...], w_ref[...].astype(BF16), preferred_element_type=F32)
    o_ref[...] = acc + b_ref[...]


def _modulation(c, w, b, layer, tn=1024):
    m, d = c.shape
    n = w.shape[2]
    return pl.pallas_call(
        _mod_kernel,
        out_shape=jax.ShapeDtypeStruct((m, n), F32),
        grid=(n // tn,),
        in_specs=[pl.BlockSpec((m, d), lambda j: (0, 0)),
                  pl.BlockSpec((None, d, tn), lambda j: (layer, 0, j)),
                  pl.BlockSpec((None, 1, tn), lambda j: (layer, 0, j))],
        out_specs=pl.BlockSpec((m, tn), lambda j: (0, j)),
        scratch_shapes=[pltpu.VMEM((m, d), BF16)],
        compiler_params=_params(("arbitrary",)),
        name="modulation",
    )(c, w, b.reshape(b.shape[0], 1, n))


NORM_ROWS = 128


def _ada_norm_to(h_ref, x_ref, gain_ref, shift_ref, scale_ref):
    tm = x_ref.shape[0]
    if shift_ref.shape[1] != 1:
        y = _rms(x_ref[...].astype(F32)) * gain_ref[...]
        h_ref[...] = (y * (1.0 + scale_ref[0]) + shift_ref[0]).astype(BF16)
        return

    def body(r, carry):
        rows = pl.ds(pl.multiple_of(r * NORM_ROWS, NORM_ROWS), NORM_ROWS)
        y = _rms(x_ref[rows, :].astype(F32)) * gain_ref[...]
        h_ref[rows, :] = (y * (1.0 + scale_ref[0]) + shift_ref[0]).astype(BF16)
        return carry

    lax.fori_loop(0, tm // NORM_ROWS, body, 0)


def _epilogue(o_ref, acc, epi, epi_scale, extra):
    tn = acc.shape[1]
    if epi is None:
        o_ref[...] = acc.astype(o_ref.dtype)
    elif epi == "rope_ret":
        cos, sin = extra[0][...], extra[1][...]
        for c in range(tn // (2 * LANES)):
            lo, mid, hi = 2 * c * LANES, (2 * c + 1) * LANES, (2 * c + 2) * LANES
            x1, x2 = acc[:, lo:mid], acc[:, mid:hi]
            o_ref[:, lo:mid] = ((x1 * cos - x2 * sin) * epi_scale).astype(o_ref.dtype)
            o_ref[:, mid:hi] = ((x2 * cos + x1 * sin) * epi_scale).astype(o_ref.dtype)
    elif epi == "rope_diff":
        cc, sa, sb = extra[0][...], extra[1][...], extra[2][...]
        for c in range(tn // LANES):
            xc = acc[:, c * LANES:(c + 1) * LANES]
            r = (xc * cc + pltpu.roll(xc, 16, 1) * sa
                 + pltpu.roll(xc, LANES - 16, 1) * sb)
            o_ref[:, c * LANES:(c + 1) * LANES] = (r * epi_scale).astype(o_ref.dtype)
    elif epi == "residual":
        res_ref, gate_ref = extra
        o_ref[...] = (res_ref[...] + gate_ref[0] * acc).astype(o_ref.dtype)


def _take(it, n):
    return [next(it) for _ in range(n)]


def _linear_kernel(*refs, has_norm, epi, epi_scale):
    n_extra = {None: 0, "rope_ret": 2, "rope_diff": 3, "residual": 2}[epi]
    it = iter(refs)
    xp_ref, xs_ref = next(it), next(it)
    if has_norm:
        gain_ref, shp_ref, scp_ref, shs_ref, scs_ref = _take(it, 5)
    w_ref = next(it)
    extra_p, extra_s = _take(it, n_extra), _take(it, n_extra)
    op_ref, os_ref = next(it), next(it)
    i, j = pl.program_id(0), pl.program_id(1)

    if has_norm:
        hp_ref, hs_ref = next(it), next(it)

        @pl.when(j == 0)
        def _():
            _ada_norm_to(hp_ref, xp_ref, gain_ref, shp_ref, scp_ref)

        @pl.when((i == 0) & (j == 0))
        def _():
            _ada_norm_to(hs_ref, xs_ref, gain_ref, shs_ref, scs_ref)

        a = hp_ref[...]
    else:
        a = xp_ref[...].astype(BF16)

    w = w_ref[...].astype(BF16)
    _epilogue(op_ref, jnp.dot(a, w, preferred_element_type=F32), epi, epi_scale, extra_p)

    @pl.when(i == 0)
    def _():
        a_s = hs_ref[...] if has_norm else xs_ref[...].astype(BF16)
        _epilogue(os_ref.at[0], jnp.dot(a_s, w, preferred_element_type=F32), epi, epi_scale,
                  extra_s)

    @pl.when(i != 0)
    def _():
        os_ref[...] = jnp.zeros_like(os_ref)


def _group_spec(arr, tm, rows_per_group, width, col_fn):
    g, r, _ = arr.shape
    if r == 1:
        return pl.BlockSpec((1, 1, width),
                            lambda i, j: ((i * tm) // rows_per_group, 0, col_fn(j)))
    return pl.BlockSpec((1, r, width), lambda i, j: (0, 0, col_fn(j)))


def _table_spec(tab, tm, rows_per_group):
    if tab.shape[0] == 1:
        return pl.BlockSpec((1, LANES), lambda i, j: (0, 0))
    nblk = rows_per_group // tm
    return pl.BlockSpec((tm, LANES), lambda i, j: (i % nblk, 0))


def _decode_out_spec(ms, tn):
    return pl.BlockSpec((1, ms, tn), lambda i, j: (i, 0, j))


def _linear(xp, xs, w, layer, *, tm, tn, out_dtype, rows_per_group, norm=None, epi=None,
            epi_p=(), epi_s=(), epi_scale=1.0, name="linear"):
    m, k = xp.shape
    ms = xs.shape[0]
    n = w.shape[2]
    has_norm = norm is not None
    args = [xp, xs]
    specs = [pl.BlockSpec((tm, k), lambda i, j: (i, 0)),
             pl.BlockSpec((ms, k), lambda i, j: (0, 0))]
    if has_norm:
        gain, shift_p, scale_p, shift_s, scale_s = norm
        args += [gain.reshape(1, k), shift_p, scale_p, shift_s, scale_s]
        specs += [pl.BlockSpec((1, k), lambda i, j: (0, 0)),
                  _group_spec(shift_p, tm, rows_per_group, k, lambda j: 0),
                  _group_spec(scale_p, tm, rows_per_group, k, lambda j: 0),
                  _group_spec(shift_s, tm, rows_per_group, k, lambda j: 0),
                  _group_spec(scale_s, tm, rows_per_group, k, lambda j: 0)]
    args.append(w)
    specs.append(pl.BlockSpec((None, k, tn), lambda i, j: (layer, 0, j)))
    if epi in ("rope_ret", "rope_diff"):
        for tab in tuple(epi_p) + tuple(epi_s):
            args.append(tab)
            specs.append(_table_spec(tab, tm, rows_per_group))
    elif epi == "residual":
        (res_p, gate_p), (res_s, gate_s) = epi_p, epi_s
        args += [res_p, gate_p, res_s, gate_s]
        specs += [pl.BlockSpec((tm, tn), lambda i, j: (i, j)),
                  _group_spec(gate_p, tm, rows_per_group, tn, lambda j: j),
                  pl.BlockSpec((ms, tn), lambda i, j: (0, j)),
                  _group_spec(gate_s, tm, rows_per_group, tn, lambda j: j)]
    out_shape = [jax.ShapeDtypeStruct((m, n), out_dtype),
                 jax.ShapeDtypeStruct((m // tm, ms, n), F32)]
    out_specs = [pl.BlockSpec((tm, tn), lambda i, j: (i, j)), _decode_out_spec(ms, tn)]
    if has_norm:
        out_shape += [jax.ShapeDtypeStruct((m, k), BF16), jax.ShapeDtypeStruct((ms, k), BF16)]
        out_specs += [pl.BlockSpec((tm, k), lambda i, j: (i, 0)),
                      pl.BlockSpec((ms, k), lambda i, j: (0, 0))]
    kern = functools.partial(_linear_kernel, has_norm=has_norm, epi=epi, epi_scale=epi_scale)
    outs = pl.pallas_call(
        kern,
        out_shape=out_shape,
        grid=(m // tm, n // tn),
        in_specs=specs,
        out_specs=out_specs,
        compiler_params=_params(("arbitrary", "arbitrary")),
        name=name,
    )(*args)
    if has_norm:
        return outs[0], outs[1][0], outs[2], outs[3]
    return outs[0], outs[1][0]


def _ffn_up_kernel(xp_ref, xs_ref, gain_ref, shp_ref, scp_ref, shs_ref, scs_ref, wa_ref, wg_ref,
                   buf_ref, cw_ref, cb_ref, u_ref, tail_ref, as_ref, us_ref,
                   hp_ref, hs_ref, carry_ref, *, tiles_per_seq):
    i, j = pl.program_id(0), pl.program_id(1)

    @pl.when(j == 0)
    def _():
        _ada_norm_to(hp_ref, xp_ref, gain_ref, shp_ref, scp_ref)

    @pl.when((i == 0) & (j == 0))
    def _():
        _ada_norm_to(hs_ref, xs_ref, gain_ref, shs_ref, scs_ref)

    @pl.when(i % tiles_per_seq == 0)
    def _():
        carry_ref[j] = jnp.zeros(carry_ref.shape[1:], F32)

    wa, wg = wa_ref[...].astype(BF16), wg_ref[...]
    h = hp_ref[...]
    a = jnp.dot(h, wa, preferred_element_type=F32)
    g = jnp.dot(h, wg, preferred_element_type=F32)
    tm, tn = a.shape
    halo = carry_ref[j]
    last = a[tm - SUBLANES:, :]
    carry_ref[j] = last
    tail_ref[0] = last
    row = lax.broadcasted_iota(jnp.int32, (tm, LANES), 0)
    for c in range(tn // LANES):
        cols = slice(c * LANES, (c + 1) * LANES)
        ac = a[:, cols]
        prev1 = halo[SUBLANES - 1:SUBLANES, cols]
        prev2 = halo[SUBLANES - 2:SUBLANES - 1, cols]
        a_m1 = jnp.where(row == 0, prev1, pltpu.roll(ac, 1, 0))
        a_m2 = jnp.where(row == 0, prev2, jnp.where(row == 1, prev1, pltpu.roll(ac, 2, 0)))
        conv = (cb_ref[:, cols] + a_m2 * cw_ref[0:1, cols] + a_m1 * cw_ref[1:2, cols]
                + ac * cw_ref[2:3, cols])
        u_ref[:, cols] = (_silu(conv) * g[:, cols]).astype(BF16)

    @pl.when(i == 0)
    def _():
        hs = hs_ref[...]
        a_s = jnp.dot(hs, wa, preferred_element_type=F32)
        g_s = jnp.dot(hs, wg, preferred_element_type=F32)
        conv = (cb_ref[...] + buf_ref[0] * cw_ref[0:1, :] + buf_ref[1] * cw_ref[1:2, :]
                + a_s * cw_ref[2:3, :])
        as_ref[0] = a_s
        us_ref[0] = (_silu(conv) * g_s).astype(BF16)

    @pl.when(i != 0)
    def _():
        as_ref[...] = jnp.zeros_like(as_ref)
        us_ref[...] = jnp.zeros_like(us_ref)


def _ffn_up(xp, xs, norm, w_up, layer, wg, buf, cw, cb, *, tm, tn, rows_per_group, name="ffn_up"):
    m, k = xp.shape
    ms = xs.shape[0]
    f = wg.shape[1]
    gain, shift_p, scale_p, shift_s, scale_s = norm
    args = [xp, xs, gain.reshape(1, k), shift_p, scale_p, shift_s, scale_s, w_up, wg, buf, cw, cb]
    specs = [pl.BlockSpec((tm, k), lambda i, j: (i, 0)),
             pl.BlockSpec((ms, k), lambda i, j: (0, 0)),
             pl.BlockSpec((1, k), lambda i, j: (0, 0)),
             _group_spec(shift_p, tm, rows_per_group, k, lambda j: 0),
             _group_spec(scale_p, tm, rows_per_group, k, lambda j: 0),
             _group_spec(shift_s, tm, rows_per_group, k, lambda j: 0),
             _group_spec(scale_s, tm, rows_per_group, k, lambda j: 0),
             pl.BlockSpec((None, k, tn), lambda i, j: (layer, 0, j)),
             pl.BlockSpec((k, tn), lambda i, j: (0, j)),
             pl.BlockSpec((CONV_W - 1, ms, tn), lambda i, j: (0, 0, j)),
             pl.BlockSpec((CONV_W, tn), lambda i, j: (0, j)),
             pl.BlockSpec((1, tn), lambda i, j: (0, j))]
    n_col = pl.cdiv(f, tn)
    out_shape = [jax.ShapeDtypeStruct((m, f), BF16),
                 jax.ShapeDtypeStruct((m // tm, SUBLANES, f), F32),
                 jax.ShapeDtypeStruct((m // tm, ms, f), F32),
                 jax.ShapeDtypeStruct((m // tm, ms, f), BF16)]
    out_specs = [pl.BlockSpec((tm, tn), lambda i, j: (i, j)),
                 pl.BlockSpec((1, SUBLANES, tn), lambda i, j: (i, 0, j)),
                 _decode_out_spec(ms, tn), _decode_out_spec(ms, tn)]
    u, tails, a_s, u_s = pl.pallas_call(
        functools.partial(_ffn_up_kernel, tiles_per_seq=rows_per_group // tm),
        out_shape=out_shape,
        grid=(m // tm, n_col),
        in_specs=specs,
        out_specs=out_specs,
        scratch_shapes=[pltpu.VMEM((tm, k), BF16), pltpu.VMEM((ms, k), BF16),
                        pltpu.VMEM((n_col, SUBLANES, tn), F32)],
        compiler_params=_params(("arbitrary", "arbitrary")),
        name=name,
    )(*args)
    return u, tails, a_s[0], u_s[0]


def _ret_prompt_kernel(sdec_ref, q_ref, k_ref, v_ref, g_ref, decay_ref, qdec_ref,
                       kdec_ref, o_ref, st_ref, *, chunk):
    sdec = sdec_ref[pl.program_id(1)]
    decay, qdec, kdec = decay_ref[0], qdec_ref[0], kdec_ref[0]
    st_ref[...] = jnp.zeros_like(st_ref)
    for n in range(q_ref.shape[0] // chunk):
        rows = slice(n * chunk, (n + 1) * chunk)
        s = st_ref[0, 0, 0]
        q, k, v = q_ref[rows, :], k_ref[rows, :], v_ref[rows, :]
        att = _dot_nt(q, k) * decay
        o = (jnp.dot(att.astype(BF16), v, preferred_element_type=F32)
             + jnp.dot(q, s.astype(BF16), preferred_element_type=F32) * qdec)
        kd = (k.astype(F32) * kdec).astype(BF16)
        st_ref[0, 0, 0] = s * sdec + _dot_tn(kd, v)
        o_ref[rows, :] = (_rms(o) * _silu(g_ref[rows, :].astype(F32))).astype(BF16)


def _ret_tables(chunk, nh):
    log_g = jnp.log1p(-jnp.exp2(-5.0 - jnp.arange(nh, dtype=F32)))
    idx = jnp.arange(chunk, dtype=F32)
    rel = idx[:, None] - idx[None, :]
    decay = jnp.where(rel >= 0, jnp.exp(jnp.maximum(rel, 0.0)[None] * log_g[:, None, None]), 0.0)
    q_dec = jnp.exp((idx[None, :] + 1.0) * log_g[:, None])[:, :, None]
    k_dec = jnp.exp((chunk - 1.0 - idx)[None, :] * log_g[:, None])[:, :, None]
    s_dec = jnp.exp(chunk * log_g)
    return decay, q_dec, k_dec, s_dec


def _retention_prompt(q, k, v, g, *, batch, seq, nh, dk, dv, chunk):
    m = q.shape[0]
    decay, q_dec, k_dec, s_dec = _ret_tables(chunk, nh)
    seq_head = lambda b, h: (b, h)
    head = lambda b, h: (h, 0, 0)
    return pl.pallas_call(
        functools.partial(_ret_prompt_kernel, chunk=chunk),
        out_shape=[jax.ShapeDtypeStruct((m, nh * dv), BF16),
                   jax.ShapeDtypeStruct((1, batch, nh, dk, dv), F32)],
        grid=(batch, nh),
        in_specs=[pl.BlockSpec(memory_space=pltpu.SMEM),
                  pl.BlockSpec((seq, dk), seq_head),
                  pl.BlockSpec((seq, dk), seq_head),
                  pl.BlockSpec((seq, dv), seq_head),
                  pl.BlockSpec((seq, dv), seq_head),
                  pl.BlockSpec((1, chunk, chunk), head),
                  pl.BlockSpec((1, chunk, 1), head),
                  pl.BlockSpec((1, chunk, 1), head)],
        out_specs=[pl.BlockSpec((seq, dv), seq_head),
                   pl.BlockSpec((1, 1, 1, dk, dv), lambda b, h: (0, b, h, 0, 0))],
        compiler_params=_params(("parallel", "parallel")),
        name="retention_prompt",
    )(s_dec, q, k, v, g, decay, q_dec, k_dec)


RET_DECODE_SEQS = 2


def _ret_decode_kernel(q_ref, k_ref, v_ref, g_ref, st_ref, dec_ref, o_ref, sto_ref, *, nh):
    decay, qdec, kdec, sdec = (dec_ref[:, i:i + 1] for i in range(4))
    rows = lax.broadcasted_iota(jnp.int32, (nh, 1), 0)
    for b in range(RET_DECODE_SEQS):
        q, k, v = q_ref[b], k_ref[b], v_ref[b]
        att = jnp.sum(q * k, axis=-1, keepdims=True) * decay
        qb, vb = q.astype(BF16), v.astype(BF16)
        kd = k * kdec
        qs = jnp.zeros(v.shape, F32)
        for h in range(nh):
            s = st_ref[0, b, h]
            qs = jnp.where(rows == h, jnp.dot(qb, s.astype(BF16), preferred_element_type=F32), qs)
            k_h = jnp.where(rows == h, kd, 0.0).astype(BF16)
            sto_ref[0, b, h] = s * sdec[h:h + 1, :] + _dot_tn(k_h, vb)
        o = att * v + qs * qdec
        o_ref[b] = _rms(o) * _silu(g_ref[b])


def _retention_decode(q, k, v, g, state, *, nh, dk, dv):
    b = q.shape[0]
    decay, q_dec, k_dec, s_dec = _ret_tables(1, nh)
    dec = jnp.stack([decay[:, 0, 0], q_dec[:, 0, 0], k_dec[:, 0, 0], s_dec], axis=1)
    tok = lambda w: pl.BlockSpec((RET_DECODE_SEQS, nh, w), lambda i: (i, 0, 0))
    st = pl.BlockSpec((1, RET_DECODE_SEQS, nh, dk, dv), lambda i: (0, i, 0, 0, 0))
    return pl.pallas_call(
        functools.partial(_ret_decode_kernel, nh=nh),
        out_shape=[jax.ShapeDtypeStruct((b, nh, dv), F32),
                   jax.ShapeDtypeStruct(state.shape, F32)],
        grid=(b // RET_DECODE_SEQS,),
        in_specs=[tok(dk), tok(dk), tok(dv), tok(dv), st,
                  pl.BlockSpec((nh, 4), lambda i: (0, 0))],
        out_specs=[tok(dv), st],
        compiler_params=_params(("parallel",)),
        name="retention_decode",
    )(q.reshape(b, nh, dk), k.reshape(b, nh, dk), v.reshape(b, nh, dv),
      g.reshape(b, nh, dv), state, dec)


def _diff_out(o, subln, out_scale):
    return _rms(o) * subln * out_scale


def _lane_tile_reduce(x, op):
    out = x[:, :LANES]
    for j in range(1, x.shape[1] // LANES):
        out = op(out, x[:, j * LANES:(j + 1) * LANES])
    return out


def _diff_prompt_kernel(lam_ref, q_ref, k_ref, v_ref, subln_ref, o_ref,
                        kb_ref, vb_ref, s_ref, p_ref, *, tq, dh):
    kb_ref[...] = k_ref[...].astype(BF16)
    vb_ref[...] = v_ref[...].astype(BF16)
    row = lax.broadcasted_iota(jnp.int32, (tq, tq), 0)
    col = lax.broadcasted_iota(jnp.int32, (tq, tq), 1)
    for qi in range(q_ref.shape[0] // tq):
        rows = slice(qi * tq, (qi + 1) * tq)
        kv_len = (qi + 1) * tq
        outs = []
        for c in range(2):
            comp = slice(c * dh, (c + 1) * dh)
            q = q_ref[rows, comp]
            m_part = None
            for ki in range(qi + 1):
                keys = slice(ki * tq, (ki + 1) * tq)
                s = _dot_nt(q, kb_ref[keys, comp])
                if ki == qi:
                    s = jnp.where(col <= row, s, -jnp.inf)
                s_ref[c, :, keys] = s
                blk_max = _lane_tile_reduce(s, jnp.maximum)
                m_part = blk_max if m_part is None else jnp.maximum(m_part, blk_max)
            m = jnp.max(m_part, axis=-1, keepdims=True)
            l_part = jnp.zeros((tq, LANES), F32)
            for ki in range(qi + 1):
                keys = slice(ki * tq, (ki + 1) * tq)
                p = jnp.exp(s_ref[c, :, keys] - m)
                l_part = l_part + _lane_tile_reduce(p, jnp.add)
                p_ref[c, :, keys] = p.astype(BF16)
            l = jnp.sum(l_part, axis=-1, keepdims=True)
            pv = jnp.dot(p_ref[c, :, :kv_len], vb_ref[:kv_len, :], preferred_element_type=F32)
            outs.append(pv / l)
        o = outs[0] - lam_ref[0] * outs[1]
        o_ref[rows, :] = _diff_out(o, subln_ref[...], lam_ref[1]).astype(BF16)


def _diff_attention_prompt(q, k, v, lam, subln, *, batch, seq, nh, dh, dv, tq=256):
    m = q.shape[0]
    seq_head = lambda b, h: (b, h)
    return pl.pallas_call(
        functools.partial(_diff_prompt_kernel, tq=tq, dh=dh),
        out_shape=jax.ShapeDtypeStruct((m, nh * dv), BF16),
        grid=(batch, nh),
        in_specs=[pl.BlockSpec(memory_space=pltpu.SMEM),
                  pl.BlockSpec((seq, 2 * dh), seq_head),
                  pl.BlockSpec((seq, 2 * dh), seq_head),
                  pl.BlockSpec((seq, dv), seq_head),
                  pl.BlockSpec((1, dv), lambda b, h: (0, 0))],
        out_specs=pl.BlockSpec((seq, dv), seq_head),
        scratch_shapes=[pltpu.VMEM((seq, 2 * dh), BF16), pltpu.VMEM((seq, dv), BF16),
                        pltpu.VMEM((2, tq, seq), F32), pltpu.VMEM((2, tq, seq), BF16)],
        compiler_params=_params(("parallel", "parallel")),
        name="diff_attention_prompt",
    )(lam, q, k, v, subln.reshape(1, dv))


PAGES_PER_STEP = 8


def _diff_decode_kernel(pt_ref, lam_ref, q_ref, kn_ref, vn_ref, subln_ref, *rest, nh, page):
    del pt_ref
    ck_refs, cv_refs = rest[:PAGES_PER_STEP], rest[PAGES_PER_STEP:2 * PAGES_PER_STEP]
    o_ref, m_ref, l_ref, acc_ref = rest[2 * PAGES_PER_STEP:]
    p = pl.program_id(1)
    rows_kv = page * nh

    @pl.when(p == 0)
    def _():
        m_ref[...] = jnp.full_like(m_ref, -jnp.inf)
        l_ref[...] = jnp.zeros_like(l_ref)
        acc_ref[...] = jnp.zeros_like(acc_ref)

    q = q_ref[0]
    qb = q.astype(BF16)
    row = lax.broadcasted_iota(jnp.int32, (2 * nh, rows_kv), 0)
    col = lax.broadcasted_iota(jnp.int32, (2 * nh, rows_kv), 1)
    own_head = (col % nh) == (row % nh)
    scores = []
    for ck_ref in ck_refs:
        k0 = ck_ref[0, pl.ds(0, rows_kv, stride=2), :].astype(BF16)
        k1 = ck_ref[0, pl.ds(1, rows_kv, stride=2), :].astype(BF16)
        s = jnp.concatenate([_dot_nt(qb[:nh], k0), _dot_nt(qb[nh:], k1)], axis=0)
        scores.append(jnp.where(own_head, s, -jnp.inf))
    m_old = m_ref[...]
    m_new = jnp.maximum(m_old, jnp.max(functools.reduce(jnp.maximum, scores), axis=-1, keepdims=True))
    alpha = jnp.exp(m_old - m_new)
    l = alpha * l_ref[...]
    acc = alpha * acc_ref[...]
    for s, cv_ref in zip(scores, cv_refs):
        pr = jnp.exp(s - m_new)
        l = l + jnp.sum(pr, axis=-1, keepdims=True)
        acc = acc + jnp.dot(pr.astype(BF16), cv_ref[0].astype(BF16), preferred_element_type=F32)
    m_ref[...] = m_new
    l_ref[...] = l
    acc_ref[...] = acc

    @pl.when(p == pl.num_programs(1) - 1)
    def _():
        s_new = jnp.sum(q * kn_ref[0], axis=-1, keepdims=True)
        m_old = m_ref[...]
        m_new = jnp.maximum(m_old, s_new)
        alpha = jnp.exp(m_old - m_new)
        pn = jnp.exp(s_new - m_new)
        l = alpha * l_ref[...] + pn
        vn = vn_ref[0]
        acc = alpha * acc_ref[...] + pn * jnp.concatenate([vn, vn], axis=0)
        on = acc / l
        o = on[:nh] - lam_ref[0] * on[nh:]
        o_ref[0] = _diff_out(o, subln_ref[...], lam_ref[1])


def _diff_attention_decode(q, k_new, v_new, cache_k, cache_v, page_table, lam, subln,
                           *, nh, dh, dv):
    b = q.shape[0]
    n_pool, page = cache_k.shape[0], cache_k.shape[1]
    n_pages = page_table.shape[1]
    ck = cache_k.reshape(n_pool, page * nh * 2, dh)
    cv = cache_v.reshape(n_pool, page * nh, dv)
    comp_major = lambda t: t.reshape(b, nh, 2, dh).transpose(0, 2, 1, 3).reshape(b, 2 * nh, dh)
    tok = lambda r, w: pl.BlockSpec((1, r, w), lambda i, p, pt: (i, 0, 0))

    def page_spec(rows, width, r):
        return pl.BlockSpec((1, rows, width),
                            lambda i, p, pt: (pt[i * n_pages + p * PAGES_PER_STEP + r], 0, 0))

    steps = n_pages // PAGES_PER_STEP
    grid_spec = pltpu.PrefetchScalarGridSpec(
        num_scalar_prefetch=1,
        grid=(b, steps),
        in_specs=([pl.BlockSpec(memory_space=pltpu.SMEM),
                   tok(2 * nh, dh), tok(2 * nh, dh), tok(nh, dv),
                   pl.BlockSpec((1, dv), lambda i, p, pt: (0, 0))]
                  + [page_spec(page * nh * 2, dh, r) for r in range(PAGES_PER_STEP)]
                  + [page_spec(page * nh, dv, r) for r in range(PAGES_PER_STEP)]),
        out_specs=tok(nh, dv),
        scratch_shapes=[pltpu.VMEM((2 * nh, 1), F32), pltpu.VMEM((2 * nh, 1), F32),
                        pltpu.VMEM((2 * nh, dv), F32)],
    )
    return pl.pallas_call(
        functools.partial(_diff_decode_kernel, nh=nh, page=page),
        out_shape=jax.ShapeDtypeStruct((b, nh, dv), F32),
        grid_spec=grid_spec,
        compiler_params=_params(("parallel", "arbitrary")),
        name="diff_attention_decode",
    )(page_table.reshape(-1), lam, comp_major(q), comp_major(k_new),
      v_new.reshape(b, nh, dv), subln.reshape(1, dv),
      *([ck] * PAGES_PER_STEP), *([cv] * PAGES_PER_STEP))


def _final_norm_kernel(x_ref, gain_ref, o_ref):
    o_ref[...] = _rms(x_ref[...]) * gain_ref[...]


def _final_norm(x, gain, tm):
    m, d = x.shape
    return pl.pallas_call(
        _final_norm_kernel,
        out_shape=jax.ShapeDtypeStruct((m, d), F32),
        grid=(m // tm,),
        in_specs=[pl.BlockSpec((tm, d), lambda i: (i, 0)),
                  pl.BlockSpec((1, d), lambda i: (0, 0))],
        out_specs=pl.BlockSpec((tm, d), lambda i: (i, 0)),
        compiler_params=_params(("parallel",)),
        name="final_norm",
    )(x, gain.reshape(1, d))


def _ret_rope_tables(pos, dk):
    half = dk // 2
    inv = jnp.float32(RET_THETA) ** (-jnp.arange(half, dtype=F32) / half)
    ang = pos.astype(F32)[:, None] * inv[None, :]
    return jnp.cos(ang), jnp.sin(ang)


def _diff_rope_tables(pos, dh, rope_dim):
    half = rope_dim // 2
    inv = jnp.float32(ROPE_THETA) ** (-jnp.arange(half, dtype=F32) / half)
    ang = pos.astype(F32)[:, None] * inv[None, :]
    cos, sin = jnp.cos(ang), jnp.sin(ang)
    n = pos.shape[0]
    rest = dh - rope_dim
    coef = jnp.concatenate([cos, cos, jnp.ones((n, rest), F32)], axis=1)
    from_lo = jnp.concatenate([jnp.zeros((n, half), F32), sin, jnp.zeros((n, rest), F32)], axis=1)
    from_hi = jnp.concatenate([-sin, jnp.zeros((n, half + rest), F32)], axis=1)
    return coef, from_lo, from_hi


def kernel(x_prompt, x_sample, state_ret, state_conv, cache_k, cache_v, page_table, c_prompt, c_sample, ret_wq, ret_wk, ret_wv, ret_wg, ret_wo, kv_norm, kv_wmod, kv_bmod, kv_wk, kv_wv, diff_wq, diff_lq1, diff_lk1, diff_lq2, diff_lk2, diff_subln, diff_wo, norm_mix, norm_ffn, w_mod, b_mod, ffn_wup, ffn_conv, ffn_conv_b, ffn_wdown, norm_final):
    bp, lp, d = x_prompt.shape
    bs, ls, _ = x_sample.shape
    assert ls == 1, "decode group carries one new token per sequence"
    depth = w_mod.shape[0]
    n_a = ret_wq.shape[0]
    nh_r, dk_r, dv_r = state_ret.shape[2], state_ret.shape[3], state_ret.shape[4]
    nh_d, dh_d, dv_d = cache_k.shape[2], cache_k.shape[4], cache_v.shape[3]
    rope_dim = dh_d // 4
    d_ff = ffn_conv.shape[2]
    past_len = page_table.shape[1] * cache_k.shape[1]
    mp = bp * lp
    tm = 1024

    pos_p = jnp.arange(lp)
    pos_s = past_len + jnp.arange(ls)
    bf = lambda w: w.astype(BF16)

    xp = x_prompt.reshape(mp, d)
    xs = x_sample.reshape(bs, d)

    n_c = bp + bs
    n_c_pad = -(-n_c // SUBLANES) * SUBLANES
    c_all = jnp.concatenate([c_prompt, c_sample, jnp.zeros((n_c_pad - n_c, d), F32)], axis=0)

    def split_mod(mod, n):
        mod_p = mod[:bp].reshape(bp, 1, n, d)
        mod_s = mod[bp:n_c].reshape(1, bs, n, d)
        return ([mod_p[:, :, i] for i in range(n)], [mod_s[:, :, i] for i in range(n)])

    lin = functools.partial(_linear, tm=tm, rows_per_group=lp)

    ret_p, ret_s, conv_p, conv_s = [], [], [], []
    k_p = v_p = k_s = v_s = None
    for l in range(depth):
        (sm_p, cm_p, gm_p, sf_p, cf_p, gf_p), (sm_s, cm_s, gm_s, sf_s, cf_s, gf_s) = split_mod(
            _modulation(c_all, w_mod, b_mod, l), 6)
        nrm_mix = (norm_mix[l], sm_p, cm_p, sm_s, cm_s)
        if l < n_a:
            k_scale = dk_r ** -0.5
            tabs_p = _ret_rope_tables(pos_p, dk_r)
            tabs_s = _ret_rope_tables(pos_s, dk_r)
            q, q_s, h_p, h_s = lin(xp, xs, ret_wq, l, tn=512, out_dtype=BF16, norm=nrm_mix, epi="rope_ret",
                                   epi_p=tabs_p, epi_s=tabs_s, name="ret_q")
            k, k_s_ = lin(h_p, h_s, ret_wk, l, tn=1024, out_dtype=BF16, epi="rope_ret", epi_p=tabs_p,
                          epi_s=tabs_s, epi_scale=k_scale, name="ret_k")
            v, v_s_ = lin(h_p, h_s, ret_wv, l, tn=1024, out_dtype=BF16, name="ret_v")
            g, g_s = lin(h_p, h_s, ret_wg, l, tn=1024, out_dtype=BF16, name="ret_g")
            o, st_p = _retention_prompt(q, k, v, g, batch=bp, seq=lp, nh=nh_r, dk=dk_r, dv=dv_r,
                                        chunk=RET_CHUNK)
            o_s, st_s = _retention_decode(q_s, k_s_, v_s_, g_s, state_ret[l:l + 1], nh=nh_r, dk=dk_r,
                                          dv=dv_r)
            xp, xs = lin(o, o_s.reshape(bs, nh_r * dv_r), ret_wo, l, tn=512, out_dtype=F32, epi="residual",
                         epi_p=(xp, gm_p), epi_s=(xs, gm_s), name="ret_o")
            ret_p.append(st_p[0])
            ret_s.append(st_s[0])
        else:
            bi = l - n_a
            tabs_p = _diff_rope_tables(pos_p, dh_d, rope_dim)
            tabs_s = _diff_rope_tables(pos_s, dh_d, rope_dim)
            if bi == 0:
                (sh_p, sc_p), (sh_s, sc_s) = split_mod(
                    _modulation(c_all, kv_wmod[None], kv_bmod[None], 0), 2)
                nrm_kv = (kv_norm, sh_p, sc_p, sh_s, sc_s)
                k_p, k_s, h_p, h_s = lin(xp, xs, kv_wk[None], 0, tn=512, out_dtype=F32, norm=nrm_kv,
                                         epi="rope_diff", epi_p=tabs_p, epi_s=tabs_s, name="kv_k")
                v_p, v_s = lin(h_p, h_s, kv_wv[None], 0, tn=1024, out_dtype=F32, name="kv_v")
            lam_init = 0.8 - 0.6 * math.exp(-0.3 * l)
            lam = (jnp.exp(jnp.sum(diff_lq1[bi].astype(F32) * diff_lk1[bi].astype(F32)))
                   - jnp.exp(jnp.sum(diff_lq2[bi].astype(F32) * diff_lk2[bi].astype(F32))) + lam_init)
            lam_sc = jnp.stack([lam, jnp.float32(1.0 - lam_init)])
            q_scale = dh_d ** -0.5
            q, q_s, _, _ = lin(xp, xs, diff_wq, bi, tn=512, out_dtype=BF16, norm=nrm_mix, epi="rope_diff",
                               epi_p=tabs_p, epi_s=tabs_s, epi_scale=q_scale, name="diff_q")
            o = _diff_attention_prompt(q, k_p, v_p, lam_sc, diff_subln[bi], batch=bp, seq=lp, nh=nh_d,
                                       dh=dh_d, dv=dv_d)
            o_s = _diff_attention_decode(q_s, k_s, v_s, cache_k, cache_v, page_table, lam_sc,
                                         diff_subln[bi], nh=nh_d, dh=dh_d, dv=dv_d)
            xp, xs = lin(o, o_s.reshape(bs, nh_d * dv_d), diff_wo, bi, tn=512, out_dtype=F32,
                         epi="residual", epi_p=(xp, gm_p), epi_s=(xs, gm_s), name="diff_o")

        wg_ = bf(ffn_wup[l][:, d_ff:])
        cw, cb = ffn_conv[l], ffn_conv_b[l].reshape(1, d_ff)
        buf = jnp.swapaxes(state_conv[l], 0, 1)
        u, tails, a_s, u_s = _ffn_up(xp, xs, (norm_ffn[l], sf_p, cf_p, sf_s, cf_s), ffn_wup, l, wg_, buf,
                                     cw, cb, tm=tm, tn=512, rows_per_group=lp, name="ffn_up")
        xp, xs = lin(u, u_s, bf(ffn_wdown[l])[None], 0, tn=512, out_dtype=F32, epi="residual",
                     epi_p=(xp, gf_p), epi_s=(xs, gf_s), name="ffn_down")
        last = tails.reshape(bp, lp // tm, SUBLANES, d_ff)[:, -1, SUBLANES - (CONV_W - 1):, :]
        conv_p.append(last)
        conv_s.append(jnp.concatenate([state_conv[l][:, 1:], a_s[:, None, :]], axis=1))

    y_prompt = _final_norm(xp, norm_final, tm).reshape(bp, lp, d)
    y_sample = _final_norm(xs, norm_final, bs).reshape(bs, ls, d)
    return (y_prompt, y_sample, jnp.stack(ret_p), jnp.stack(ret_s), jnp.stack(conv_p), jnp.stack(conv_s),
            k_p.reshape(bp, lp, nh_d, 2, dh_d), v_p.reshape(bp, lp, nh_d, dv_d),
            k_s.reshape(bs, ls, nh_d, 2, dh_d), v_s.reshape(bs, ls, nh_d, dv_d))
```

```python
import functools
import math

import jax
import jax.numpy as jnp
from jax import lax
from jax.experimental import pallas as pl
from jax.experimental.pallas import tpu as pltpu

F32 = jnp.float32
BF16 = jnp.bfloat16

NORM_EPS = 1e-6
RET_THETA = 10000.0
ROPE_THETA = 500000.0
RET_CHUNK = 256
CONV_W = 3

V7X_VMEM_LIMIT_BYTES = 56 * 1024 * 1024
LANES = 128
SUBLANES = 8


def _params(semantics):
    return pltpu.CompilerParams(dimension_semantics=semantics,
                                vmem_limit_bytes=V7X_VMEM_LIMIT_BYTES)


def _silu(x):
    return x * jax.nn.sigmoid(x)


def _rms(x):
    return x * lax.rsqrt(jnp.mean(x * x, axis=-1, keepdims=True) + NORM_EPS)


def _dot_nt(a, b):
    return lax.dot_general(a, b, (((1,), (1,)), ((), ())), preferred_element_type=F32)


def _dot_tn(a, b):
    return lax.dot_general(a, b, (((0,), (0,)), ((), ())), preferred_element_type=F32)


def _mod_kernel(c_ref, w_ref, b_ref, o_ref, s_ref):
    @pl.when(pl.program_id(0) == 0)
    def _():
        s_ref[...] = _silu(c_ref[...]).astype(BF16)

    acc = jnp.dot(s_ref[...], w_ref[...].astype(BF16), preferred_element_type=F32)
    o_ref[...] = acc + b_ref[...]


def _modulation(c, w, b, layer, tn=1024):
    m, d = c.shape
    n = w.shape[2]
    return pl.pallas_call(
        _mod_kernel,
        out_shape=jax.ShapeDtypeStruct((m, n), F32),
        grid=(n // tn,),
        in_specs=[pl.BlockSpec((m, d), lambda j: (0, 0)),
                  pl.BlockSpec((None, d, tn), lambda j: (layer, 0, j)),
                  pl.BlockSpec((None, 1, tn), lambda j: (layer, 0, j))],
        out_specs=pl.BlockSpec((m, tn), lambda j: (0, j)),
        scratch_shapes=[pltpu.VMEM((m, d), BF16)],
        compiler_params=_params(("arbitrary",)),
        name="modulation",
    )(c, w, b.reshape(b.shape[0], 1, n))


NORM_ROWS = 128


def _ada_norm_to(h_ref, x_ref, gain_ref, shift_ref, scale_ref):
    tm = x_ref.shape[0]
    if shift_ref.shape[1] != 1:
        y = _rms(x_ref[...].astype(F32)) * gain_ref[...]
        h_ref[...] = (y * (1.0 + scale_ref[0]) + shift_ref[0]).astype(BF16)
        return

    def body(r, carry):
        rows = pl.ds(pl.multiple_of(r * NORM_ROWS, NORM_ROWS), NORM_ROWS)
        y = _rms(x_ref[rows, :].astype(F32)) * gain_ref[...]
        h_ref[rows, :] = (y * (1.0 + scale_ref[0]) + shift_ref[0]).astype(BF16)
        return carry

    lax.fori_loop(0, tm // NORM_ROWS, body, 0)


def _epilogue(o_ref, acc, epi, epi_scale, extra):
    tn = acc.shape[1]
    if epi is None:
        o_ref[...] = acc.astype(o_ref.dtype)
    elif epi == "rope_ret":
        cos, sin = extra[0][...], extra[1][...]
        for c in range(tn // (2 * LANES)):
            lo, mid, hi = 2 * c * LANES, (2 * c + 1) * LANES, (2 * c + 2) * LANES
            x1, x2 = acc[:, lo:mid], acc[:, mid:hi]
            o_ref[:, lo:mid] = ((x1 * cos - x2 * sin) * epi_scale).astype(o_ref.dtype)
            o_ref[:, mid:hi] = ((x2 * cos + x1 * sin) * epi_scale).astype(o_ref.dtype)
    elif epi == "rope_diff":
        cc, sa, sb = extra[0][...], extra[1][...], extra[2][...]
        for c in range(tn // LANES):
            xc = acc[:, c * LANES:(c + 1) * LANES]
            r = (xc * cc + pltpu.roll(xc, 16, 1) * sa
                 + pltpu.roll(xc, LANES - 16, 1) * sb)
            o_ref[:, c * LANES:(c + 1) * LANES] = (r * epi_scale).astype(o_ref.dtype)
    elif epi == "residual":
        res_ref, gate_ref = extra
        o_ref[...] = (res_ref[...] + gate_ref[0] * acc).astype(o_ref.dtype)


def _take(it, n):
    return [next(it) for _ in range(n)]


def _linear_kernel(*refs, has_norm, epi, epi_scale):
    n_extra = {None: 0, "rope_ret": 2, "rope_diff": 3, "residual": 2}[epi]
    it = iter(refs)
    xp_ref, xs_ref = next(it), next(it)
    if has_norm:
        gain_ref, shp_ref, scp_ref, shs_ref, scs_ref = _take(it, 5)
    w_ref = next(it)
    extra_p, extra_s = _take(it, n_extra), _take(it, n_extra)
    op_ref, os_ref = next(it), next(it)
    i, j = pl.program_id(0), pl.program_id(1)

    if has_norm:
        hp_ref, hs_ref = next(it), next(it)

        @pl.when(j == 0)
        def _():
            _ada_norm_to(hp_ref, xp_ref, gain_ref, shp_ref, scp_ref)

        @pl.when((i == 0) & (j == 0))
        def _():
            _ada_norm_to(hs_ref, xs_ref, gain_ref, shs_ref, scs_ref)

        a = hp_ref[...]
    else:
        a = xp_ref[...].astype(BF16)

    w = w_ref[...].astype(BF16)
    _epilogue(op_ref, jnp.dot(a, w, preferred_element_type=F32), epi, epi_scale, extra_p)

    @pl.when(i == 0)
    def _():
        a_s = hs_ref[...] if has_norm else xs_ref[...].astype(BF16)
        _epilogue(os_ref.at[0], jnp.dot(a_s, w, preferred_element_type=F32), epi, epi_scale,
                  extra_s)

    @pl.when(i != 0)
    def _():
        os_ref[...] = jnp.zeros_like(os_ref)


def _group_spec(arr, tm, rows_per_group, width, col_fn):
    g, r, _ = arr.shape
    if r == 1:
        return pl.BlockSpec((1, 1, width),
                            lambda i, j: ((i * tm) // rows_per_group, 0, col_fn(j)))
    return pl.BlockSpec((1, r, width), lambda i, j: (0, 0, col_fn(j)))


def _table_spec(tab, tm, rows_per_group):
    if tab.shape[0] == 1:
        return pl.BlockSpec((1, LANES), lambda i, j: (0, 0))
    nblk = rows_per_group // tm
    return pl.BlockSpec((tm, LANES), lambda i, j: (i % nblk, 0))


def _decode_out_spec(ms, tn):
    return pl.BlockSpec((1, ms, tn), lambda i, j: (i, 0, j))


def _linear(xp, xs, w, layer, *, tm, tn, out_dtype, rows_per_group, norm=None, epi=None,
            epi_p=(), epi_s=(), epi_scale=1.0, name="linear"):
    m, k = xp.shape
    ms = xs.shape[0]
    n = w.shape[2]
    has_norm = norm is not None
    args = [xp, xs]
    specs = [pl.BlockSpec((tm, k), lambda i, j: (i, 0)),
             pl.BlockSpec((ms, k), lambda i, j: (0, 0))]
    if has_norm:
        gain, shift_p, scale_p, shift_s, scale_s = norm
        args += [gain.reshape(1, k), shift_p, scale_p, shift_s, scale_s]
        specs += [pl.BlockSpec((1, k), lambda i, j: (0, 0)),
                  _group_spec(shift_p, tm, rows_per_group, k, lambda j: 0),
                  _group_spec(scale_p, tm, rows_per_group, k, lambda j: 0),
                  _group_spec(shift_s, tm, rows_per_group, k, lambda j: 0),
                  _group_spec(scale_s, tm, rows_per_group, k, lambda j: 0)]
    args.append(w)
    specs.append(pl.BlockSpec((None, k, tn), lambda i, j: (layer, 0, j)))
    if epi in ("rope_ret", "rope_diff"):
        for tab in tuple(epi_p) + tuple(epi_s):
            args.append(tab)
            specs.append(_table_spec(tab, tm, rows_per_group))
    elif epi == "residual":
        (res_p, gate_p), (res_s, gate_s) = epi_p, epi_s
        args += [res_p, gate_p, res_s, gate_s]
        specs += [pl.BlockSpec((tm, tn), lambda i, j: (i, j)),
                  _group_spec(gate_p, tm, rows_per_group, tn, lambda j: j),
                  pl.BlockSpec((ms, tn), lambda i, j: (0, j)),
                  _group_spec(gate_s, tm, rows_per_group, tn, lambda j: j)]
    out_shape = [jax.ShapeDtypeStruct((m, n), out_dtype),
                 jax.ShapeDtypeStruct((m // tm, ms, n), F32)]
    out_specs = [pl.BlockSpec((tm, tn), lambda i, j: (i, j)), _decode_out_spec(ms, tn)]
    if has_norm:
        out_shape += [jax.ShapeDtypeStruct((m, k), BF16), jax.ShapeDtypeStruct((ms, k), BF16)]
        out_specs += [pl.BlockSpec((tm, k), lambda i, j: (i, 0)),
                      pl.BlockSpec((ms, k), lambda i, j: (0, 0))]
    kern = functools.partial(_linear_kernel, has_norm=has_norm, epi=epi, epi_scale=epi_scale)
    outs = pl.pallas_call(
        kern,
        out_shape=out_shape,
        grid=(m // tm, n // tn),
        in_specs=specs,
        out_specs=out_specs,
        compiler_params=_params(("arbitrary", "arbitrary")),
        name=name,
    )(*args)
    if has_norm:
        return outs[0], outs[1][0], outs[2], outs[3]
    return outs[0], outs[1][0]


def _ffn_up_kernel(xp_ref, xs_ref, gain_ref, shp_ref, scp_ref, shs_ref, scs_ref, wa_ref, wg_ref,
                   buf_ref, cw_ref, cb_ref, u_ref, tail_ref, as_ref, us_ref,
                   hp_ref, hs_ref, carry_ref, *, tiles_per_seq):
    i, j = pl.program_id(0), pl.program_id(1)

    @pl.when(j == 0)
    def _():
        _ada_norm_to(hp_ref, xp_ref, gain_ref, shp_ref, scp_ref)

    @pl.when((i == 0) & (j == 0))
    def _():
        _ada_norm_to(hs_ref, xs_ref, gain_ref, shs_ref, scs_ref)

    @pl.when(i % tiles_per_seq == 0)
    def _():
        carry_ref[j] = jnp.zeros(carry_ref.shape[1:], F32)

    wa, wg = wa_ref[...], wg_ref[...]
    h = hp_ref[...]
    a = jnp.dot(h, wa, preferred_element_type=F32)
    g = jnp.dot(h, wg, preferred_element_type=F32)
    tm, tn = a.shape
    halo = carry_ref[j]
    last = a[tm - SUBLANES:, :]
    carry_ref[j] = last
    tail_ref[0] = last
    row = lax.broadcasted_iota(jnp.int32, (tm, LANES), 0)
    for c in range(tn // LANES):
        cols = slice(c * LANES, (c + 1) * LANES)
        ac = a[:, cols]
        prev1 = halo[SUBLANES - 1:SUBLANES, cols]
        prev2 = halo[SUBLANES - 2:SUBLANES - 1, cols]
        a_m1 = jnp.where(row == 0, prev1, pltpu.roll(ac, 1, 0))
        a_m2 = jnp.where(row == 0, prev2, jnp.where(row == 1, prev1, pltpu.roll(ac, 2, 0)))
        conv = (cb_ref[:, cols] + a_m2 * cw_ref[0:1, cols] + a_m1 * cw_ref[1:2, cols]
                + ac * cw_ref[2:3, cols])
        u_ref[:, cols] = (_silu(conv) * g[:, cols]).astype(BF16)

    @pl.when(i == 0)
    def _():
        hs = hs_ref[...]
        a_s = jnp.dot(hs, wa, preferred_element_type=F32)
        g_s = jnp.dot(hs, wg, preferred_element_type=F32)
        conv = (cb_ref[...] + buf_ref[0] * cw_ref[0:1, :] + buf_ref[1] * cw_ref[1:2, :]
                + a_s * cw_ref[2:3, :])
        as_ref[0] = a_s
        us_ref[0] = (_silu(conv) * g_s).astype(BF16)

    @pl.when(i != 0)
    def _():
        as_ref[...] = jnp.zeros_like(as_ref)
        us_ref[...] = jnp.zeros_like(us_ref)


def _ffn_up(xp, xs, norm, wa, wg, buf, cw, cb, *, tm, tn, rows_per_group, name="ffn_up"):
    m, k = xp.shape
    ms = xs.shape[0]
    f = wg.shape[1]
    gain, shift_p, scale_p, shift_s, scale_s = norm
    args = [xp, xs, gain.reshape(1, k), shift_p, scale_p, shift_s, scale_s, wa, wg, buf, cw, cb]
    specs = [pl.BlockSpec((tm, k), lambda i, j: (i, 0)),
             pl.BlockSpec((ms, k), lambda i, j: (0, 0)),
             pl.BlockSpec((1, k), lambda i, j: (0, 0)),
             _group_spec(shift_p, tm, rows_per_group, k, lambda j: 0),
             _group_spec(scale_p, tm, rows_per_group, k, lambda j: 0),
             _group_spec(shift_s, tm, rows_per_group, k, lambda j: 0),
             _group_spec(scale_s, tm, rows_per_group, k, lambda j: 0),
             pl.BlockSpec((k, tn), lambda i, j: (0, j)),
             pl.BlockSpec((k, tn), lambda i, j: (0, j)),
             pl.BlockSpec((CONV_W - 1, ms, tn), lambda i, j: (0, 0, j)),
             pl.BlockSpec((CONV_W, tn), lambda i, j: (0, j)),
             pl.BlockSpec((1, tn), lambda i, j: (0, j))]
    n_col = pl.cdiv(f, tn)
    out_shape = [jax.ShapeDtypeStruct((m, f), BF16),
                 jax.ShapeDtypeStruct((m // tm, SUBLANES, f), F32),
                 jax.ShapeDtypeStruct((m // tm, ms, f), F32),
                 jax.ShapeDtypeStruct((m // tm, ms, f), BF16)]
    out_specs = [pl.BlockSpec((tm, tn), lambda i, j: (i, j)),
                 pl.BlockSpec((1, SUBLANES, tn), lambda i, j: (i, 0, j)),
                 _decode_out_spec(ms, tn), _decode_out_spec(ms, tn)]
    u, tails, a_s, u_s = pl.pallas_call(
        functools.partial(_ffn_up_kernel, tiles_per_seq=rows_per_group // tm),
        out_shape=out_shape,
        grid=(m // tm, n_col),
        in_specs=specs,
        out_specs=out_specs,
        scratch_shapes=[pltpu.VMEM((tm, k), BF16), pltpu.VMEM((ms, k), BF16),
                        pltpu.VMEM((n_col, SUBLANES, tn), F32)],
        compiler_params=_params(("arbitrary", "arbitrary")),
        name=name,
    )(*args)
    return u, tails, a_s[0], u_s[0]


def _narrow_kernel(w_ref, *o_refs):
    width = o_refs[0].shape[1]
    for t, o_ref in enumerate(o_refs):
        o_ref[...] = w_ref[:, t * width:(t + 1) * width].astype(BF16)


def _narrow(w, layer, parts, rows):
    _, k, n = w.shape
    width = n // parts
    return pl.pallas_call(
        _narrow_kernel,
        out_shape=[jax.ShapeDtypeStruct((k, width), BF16)] * parts,
        grid=(k // rows,),
        in_specs=[pl.BlockSpec((None, rows, n), lambda r: (layer, r, 0))],
        out_specs=[pl.BlockSpec((rows, width), lambda r: (r, 0))] * parts,
        compiler_params=_params(("parallel",)),
        name="narrow_weights",
    )(w)


def _ret_prompt_kernel(sdec_ref, q_ref, k_ref, v_ref, g_ref, decay_ref, qdec_ref,
                       kdec_ref, o_ref, st_ref, *, chunk):
    sdec = sdec_ref[pl.program_id(1)]
    decay, qdec, kdec = decay_ref[0], qdec_ref[0], kdec_ref[0]
    st_ref[...] = jnp.zeros_like(st_ref)
    for n in range(q_ref.shape[0] // chunk):
        rows = slice(n * chunk, (n + 1) * chunk)
        s = st_ref[0, 0, 0]
        q, k, v = q_ref[rows, :], k_ref[rows, :], v_ref[rows, :]
        att = _dot_nt(q, k) * decay
        o = (jnp.dot(att.astype(BF16), v, preferred_element_type=F32)
             + jnp.dot(q, s.astype(BF16), preferred_element_type=F32) * qdec)
        kd = (k.astype(F32) * kdec).astype(BF16)
        st_ref[0, 0, 0] = s * sdec + _dot_tn(kd, v)
        o_ref[rows, :] = (_rms(o) * _silu(g_ref[rows, :].astype(F32))).astype(BF16)


def _ret_tables(chunk, nh):
    log_g = jnp.log1p(-jnp.exp2(-5.0 - jnp.arange(nh, dtype=F32)))
    idx = jnp.arange(chunk, dtype=F32)
    rel = idx[:, None] - idx[None, :]
    decay = jnp.where(rel >= 0, jnp.exp(jnp.maximum(rel, 0.0)[None] * log_g[:, None, None]), 0.0)
    q_dec = jnp.exp((idx[None, :] + 1.0) * log_g[:, None])[:, :, None]
    k_dec = jnp.exp((chunk - 1.0 - idx)[None, :] * log_g[:, None])[:, :, None]
    s_dec = jnp.exp(chunk * log_g)
    return decay, q_dec, k_dec, s_dec


def _retention_prompt(q, k, v, g, *, batch, seq, nh, dk, dv, chunk):
    m = q.shape[0]
    decay, q_dec, k_dec, s_dec = _ret_tables(chunk, nh)
    seq_head = lambda b, h: (b, h)
    head = lambda b, h: (h, 0, 0)
    return pl.pallas_call(
        functools.partial(_ret_prompt_kernel, chunk=chunk),
        out_shape=[jax.ShapeDtypeStruct((m, nh * dv), BF16),
                   jax.ShapeDtypeStruct((1, batch, nh, dk, dv), F32)],
        grid=(batch, nh),
        in_specs=[pl.BlockSpec(memory_space=pltpu.SMEM),
                  pl.BlockSpec((seq, dk), seq_head),
                  pl.BlockSpec((seq, dk), seq_head),
                  pl.BlockSpec((seq, dv), seq_head),
                  pl.BlockSpec((seq, dv), seq_head),
                  pl.BlockSpec((1, chunk, chunk), head),
                  pl.BlockSpec((1, chunk, 1), head),
                  pl.BlockSpec((1, chunk, 1), head)],
        out_specs=[pl.BlockSpec((seq, dv), seq_head),
                   pl.BlockSpec((1, 1, 1, dk, dv), lambda b, h: (0, b, h, 0, 0))],
        compiler_params=_params(("parallel", "parallel")),
        name="retention_prompt",
    )(s_dec, q, k, v, g, decay, q_dec, k_dec)


RET_DECODE_SEQS = 2


def _ret_decode_kernel(q_ref, k_ref, v_ref, g_ref, st_ref, dec_ref, o_ref, sto_ref, *, nh):
    decay, qdec, kdec, sdec = (dec_ref[:, i:i + 1] for i in range(4))
    rows = lax.broadcasted_iota(jnp.int32, (nh, 1), 0)
    for b in range(RET_DECODE_SEQS):
        q, k, v = q_ref[b], k_ref[b], v_ref[b]
        att = jnp.sum(q * k, axis=-1, keepdims=True) * decay
        qb, vb = q.astype(BF16), v.astype(BF16)
        kd = k * kdec
        qs = jnp.zeros(v.shape, F32)
        for h in range(nh):
            s = st_ref[0, b, h]
            qs = jnp.where(rows == h, jnp.dot(qb, s.astype(BF16), preferred_element_type=F32), qs)
            k_h = jnp.where(rows == h, kd, 0.0).astype(BF16)
            sto_ref[0, b, h] = s * sdec[h:h + 1, :] + _dot_tn(k_h, vb)
        o = att * v + qs * qdec
        o_ref[b] = _rms(o) * _silu(g_ref[b])


def _retention_decode(q, k, v, g, state, *, nh, dk, dv):
    b = q.shape[0]
    decay, q_dec, k_dec, s_dec = _ret_tables(1, nh)
    dec = jnp.stack([decay[:, 0, 0], q_dec[:, 0, 0], k_dec[:, 0, 0], s_dec], axis=1)
    tok = lambda w: pl.BlockSpec((RET_DECODE_SEQS, nh, w), lambda i: (i, 0, 0))
    st = pl.BlockSpec((1, RET_DECODE_SEQS, nh, dk, dv), lambda i: (0, i, 0, 0, 0))
    return pl.pallas_call(
        functools.partial(_ret_decode_kernel, nh=nh),
        out_shape=[jax.ShapeDtypeStruct((b, nh, dv), F32),
                   jax.ShapeDtypeStruct(state.shape, F32)],
        grid=(b // RET_DECODE_SEQS,),
        in_specs=[tok(dk), tok(dk), tok(dv), tok(dv), st,
                  pl.BlockSpec((nh, 4), lambda i: (0, 0))],
        out_specs=[tok(dv), st],
        compiler_params=_params(("parallel",)),
        name="retention_decode",
    )(q.reshape(b, nh, dk), k.reshape(b, nh, dk), v.reshape(b, nh, dv),
      g.reshape(b, nh, dv), state, dec)


def _diff_out(o, subln, out_scale):
    return _rms(o) * subln * out_scale


def _lane_tile_reduce(x, op):
    out = x[:, :LANES]
    for j in range(1, x.shape[1] // LANES):
        out = op(out, x[:, j * LANES:(j + 1) * LANES])
    return out


def _diff_prompt_kernel(lam_ref, q_ref, k_ref, v_ref, subln_ref, o_ref,
                        kb_ref, vb_ref, s_ref, p_ref, *, tq, dh):
    kb_ref[...] = k_ref[...].astype(BF16)
    vb_ref[...] = v_ref[...].astype(BF16)
    row = lax.broadcasted_iota(jnp.int32, (tq, tq), 0)
    col = lax.broadcasted_iota(jnp.int32, (tq, tq), 1)
    for qi in range(q_ref.shape[0] // tq):
        rows = slice(qi * tq, (qi + 1) * tq)
        kv_len = (qi + 1) * tq
        outs = []
        for c in range(2):
            comp = slice(c * dh, (c + 1) * dh)
            q = q_ref[rows, comp]
            m_part = None
            for ki in range(qi + 1):
                keys = slice(ki * tq, (ki + 1) * tq)
                s = _dot_nt(q, kb_ref[keys, comp])
                if ki == qi:
                    s = jnp.where(col <= row, s, -jnp.inf)
                s_ref[c, :, keys] = s
                blk_max = _lane_tile_reduce(s, jnp.maximum)
                m_part = blk_max if m_part is None else jnp.maximum(m_part, blk_max)
            m = jnp.max(m_part, axis=-1, keepdims=True)
            l_part = jnp.zeros((tq, LANES), F32)
            for ki in range(qi + 1):
                keys = slice(ki * tq, (ki + 1) * tq)
                p = jnp.exp(s_ref[c, :, keys] - m)
                l_part = l_part + _lane_tile_reduce(p, jnp.add)
                p_ref[c, :, keys] = p.astype(BF16)
            l = jnp.sum(l_part, axis=-1, keepdims=True)
            pv = jnp.dot(p_ref[c, :, :kv_len], vb_ref[:kv_len, :], preferred_element_type=F32)
            outs.append(pv / l)
        o = outs[0] - lam_ref[0] * outs[1]
        o_ref[rows, :] = _diff_out(o, subln_ref[...], lam_ref[1]).astype(BF16)


def _diff_attention_prompt(q, k, v, lam, subln, *, batch, seq, nh, dh, dv, tq=256):
    m = q.shape[0]
    seq_head = lambda b, h: (b, h)
    return pl.pallas_call(
        functools.partial(_diff_prompt_kernel, tq=tq, dh=dh),
        out_shape=jax.ShapeDtypeStruct((m, nh * dv), BF16),
        grid=(batch, nh),
        in_specs=[pl.BlockSpec(memory_space=pltpu.SMEM),
                  pl.BlockSpec((seq, 2 * dh), seq_head),
                  pl.BlockSpec((seq, 2 * dh), seq_head),
                  pl.BlockSpec((seq, dv), seq_head),
                  pl.BlockSpec((1, dv), lambda b, h: (0, 0))],
        out_specs=pl.BlockSpec((seq, dv), seq_head),
        scratch_shapes=[pltpu.VMEM((seq, 2 * dh), BF16), pltpu.VMEM((seq, dv), BF16),
                        pltpu.VMEM((2, tq, seq), F32), pltpu.VMEM((2, tq, seq), BF16)],
        compiler_params=_params(("parallel", "parallel")),
        name="diff_attention_prompt",
    )(lam, q, k, v, subln.reshape(1, dv))


PAGES_PER_STEP = 8


def _diff_decode_kernel(pt_ref, lam_ref, q_ref, kn_ref, vn_ref, subln_ref, *rest, nh, page):
    del pt_ref
    ck_refs, cv_refs = rest[:PAGES_PER_STEP], rest[PAGES_PER_STEP:2 * PAGES_PER_STEP]
    o_ref, m_ref, l_ref, acc_ref = rest[2 * PAGES_PER_STEP:]
    p = pl.program_id(1)
    rows_kv = page * nh

    @pl.when(p == 0)
    def _():
        m_ref[...] = jnp.full_like(m_ref, -jnp.inf)
        l_ref[...] = jnp.zeros_like(l_ref)
        acc_ref[...] = jnp.zeros_like(acc_ref)

    q = q_ref[0]
    qb = q.astype(BF16)
    row = lax.broadcasted_iota(jnp.int32, (2 * nh, rows_kv), 0)
    col = lax.broadcasted_iota(jnp.int32, (2 * nh, rows_kv), 1)
    own_head = (col % nh) == (row % nh)
    scores = []
    for ck_ref in ck_refs:
        k0 = ck_ref[0, pl.ds(0, rows_kv, stride=2), :].astype(BF16)
        k1 = ck_ref[0, pl.ds(1, rows_kv, stride=2), :].astype(BF16)
        s = jnp.concatenate([_dot_nt(qb[:nh], k0), _dot_nt(qb[nh:], k1)], axis=0)
        scores.append(jnp.where(own_head, s, -jnp.inf))
    m_old = m_ref[...]
    m_new = jnp.maximum(m_old, jnp.max(functools.reduce(jnp.maximum, scores), axis=-1, keepdims=True))
    alpha = jnp.exp(m_old - m_new)
    l = alpha * l_ref[...]
    acc = alpha * acc_ref[...]
    for s, cv_ref in zip(scores, cv_refs):
        pr = jnp.exp(s - m_new)
        l = l + jnp.sum(pr, axis=-1, keepdims=True)
        acc = acc + jnp.dot(pr.astype(BF16), cv_ref[0].astype(BF16), preferred_element_type=F32)
    m_ref[...] = m_new
    l_ref[...] = l
    acc_ref[...] = acc

    @pl.when(p == pl.num_programs(1) - 1)
    def _():
        s_new = jnp.sum(q * kn_ref[0], axis=-1, keepdims=True)
        m_old = m_ref[...]
        m_new = jnp.maximum(m_old, s_new)
        alpha = jnp.exp(m_old - m_new)
        pn = jnp.exp(s_new - m_new)
        l = alpha * l_ref[...] + pn
        vn = vn_ref[0]
        acc = alpha * acc_ref[...] + pn * jnp.concatenate([vn, vn], axis=0)
        on = acc / l
        o = on[:nh] - lam_ref[0] * on[nh:]
        o_ref[0] = _diff_out(o, subln_ref[...], lam_ref[1])


def _diff_attention_decode(q, k_new, v_new, cache_k, cache_v, page_table, lam, subln,
                           *, nh, dh, dv):
    b = q.shape[0]
    n_pool, page = cache_k.shape[0], cache_k.shape[1]
    n_pages = page_table.shape[1]
    ck = cache_k.reshape(n_pool, page * nh * 2, dh)
    cv = cache_v.reshape(n_pool, page * nh, dv)
    comp_major = lambda t: t.reshape(b, nh, 2, dh).transpose(0, 2, 1, 3).reshape(b, 2 * nh, dh)
    tok = lambda r, w: pl.BlockSpec((1, r, w), lambda i, p, pt: (i, 0, 0))

    def page_spec(rows, width, r):
        return pl.BlockSpec((1, rows, width),
                            lambda i, p, pt: (pt[i * n_pages + p * PAGES_PER_STEP + r], 0, 0))

    steps = n_pages // PAGES_PER_STEP
    grid_spec = pltpu.PrefetchScalarGridSpec(
        num_scalar_prefetch=1,
        grid=(b, steps),
        in_specs=([pl.BlockSpec(memory_space=pltpu.SMEM),
                   tok(2 * nh, dh), tok(2 * nh, dh), tok(nh, dv),
                   pl.BlockSpec((1, dv), lambda i, p, pt: (0, 0))]
                  + [page_spec(page * nh * 2, dh, r) for r in range(PAGES_PER_STEP)]
                  + [page_spec(page * nh, dv, r) for r in range(PAGES_PER_STEP)]),
        out_specs=tok(nh, dv),
        scratch_shapes=[pltpu.VMEM((2 * nh, 1), F32), pltpu.VMEM((2 * nh, 1), F32),
                        pltpu.VMEM((2 * nh, dv), F32)],
    )
    return pl.pallas_call(
        functools.partial(_diff_decode_kernel, nh=nh, page=page),
        out_shape=jax.ShapeDtypeStruct((b, nh, dv), F32),
        grid_spec=grid_spec,
        compiler_params=_params(("parallel", "arbitrary")),
        name="diff_attention_decode",
    )(page_table.reshape(-1), lam, comp_major(q), comp_major(k_new),
      v_new.reshape(b, nh, dv), subln.reshape(1, dv),
      *([ck] * PAGES_PER_STEP), *([cv] * PAGES_PER_STEP))


def _final_norm_kernel(x_ref, gain_ref, o_ref):
    o_ref[...] = _rms(x_ref[...]) * gain_ref[...]


def _final_norm(x, gain, tm):
    m, d = x.shape
    return pl.pallas_call(
        _final_norm_kernel,
        out_shape=jax.ShapeDtypeStruct((m, d), F32),
        grid=(m // tm,),
        in_specs=[pl.BlockSpec((tm, d), lambda i: (i, 0)),
                  pl.BlockSpec((1, d), lambda i: (0, 0))],
        out_specs=pl.BlockSpec((tm, d), lambda i: (i, 0)),
        compiler_params=_params(("parallel",)),
        name="final_norm",
    )(x, gain.reshape(1, d))


def _ret_rope_tables(pos, dk):
    half = dk // 2
    inv = jnp.float32(RET_THETA) ** (-jnp.arange(half, dtype=F32) / half)
    ang = pos.astype(F32)[:, None] * inv[None, :]
    return jnp.cos(ang), jnp.sin(ang)


def _diff_rope_tables(pos, dh, rope_dim):
    half = rope_dim // 2
    inv = jnp.float32(ROPE_THETA) ** (-jnp.arange(half, dtype=F32) / half)
    ang = pos.astype(F32)[:, None] * inv[None, :]
    cos, sin = jnp.cos(ang), jnp.sin(ang)
    n = pos.shape[0]
    rest = dh - rope_dim
    coef = jnp.concatenate([cos, cos, jnp.ones((n, rest), F32)], axis=1)
    from_lo = jnp.concatenate([jnp.zeros((n, half), F32), sin, jnp.zeros((n, rest), F32)], axis=1)
    from_hi = jnp.concatenate([-sin, jnp.zeros((n, half + rest), F32)], axis=1)
    return coef, from_lo, from_hi


def kernel(x_prompt, x_sample, state_ret, state_conv, cache_k, cache_v, page_table, c_prompt, c_sample, ret_wq, ret_wk, ret_wv, ret_wg, ret_wo, kv_norm, kv_wmod, kv_bmod, kv_wk, kv_wv, diff_wq, diff_lq1, diff_lk1, diff_lq2, diff_lk2, diff_subln, diff_wo, norm_mix, norm_ffn, w_mod, b_mod, ffn_wup, ffn_conv, ffn_conv_b, ffn_wdown, norm_final):
    bp, lp, d = x_prompt.shape
    bs, ls, _ = x_sample.shape
    assert ls == 1, "decode group carries one new token per sequence"
    depth = w_mod.shape[0]
    n_a = ret_wq.shape[0]
    nh_r, dk_r, dv_r = state_ret.shape[2], state_ret.shape[3], state_ret.shape[4]
    nh_d, dh_d, dv_d = cache_k.shape[2], cache_k.shape[4], cache_v.shape[3]
    rope_dim = dh_d // 4
    d_ff = ffn_conv.shape[2]
    past_len = page_table.shape[1] * cache_k.shape[1]
    mp = bp * lp
    tm = 1024

    pos_p = jnp.arange(lp)
    pos_s = past_len + jnp.arange(ls)

    xp = x_prompt.reshape(mp, d)
    xs = x_sample.reshape(bs, d)

    n_c = bp + bs
    n_c_pad = -(-n_c // SUBLANES) * SUBLANES
    c_all = jnp.concatenate([c_prompt, c_sample, jnp.zeros((n_c_pad - n_c, d), F32)], axis=0)

    def split_mod(mod, n):
        mod_p = mod[:bp].reshape(bp, 1, n, d)
        mod_s = mod[bp:n_c].reshape(1, bs, n, d)
        return ([mod_p[:, :, i] for i in range(n)], [mod_s[:, :, i] for i in range(n)])

    lin = functools.partial(_linear, tm=tm, rows_per_group=lp)

    ret_p, ret_s, conv_p, conv_s = [], [], [], []
    k_p = v_p = k_s = v_s = None
    for l in range(depth):
        (sm_p, cm_p, gm_p, sf_p, cf_p, gf_p), (sm_s, cm_s, gm_s, sf_s, cf_s, gf_s) = split_mod(
            _modulation(c_all, w_mod, b_mod, l), 6)
        nrm_mix = (norm_mix[l], sm_p, cm_p, sm_s, cm_s)
        if l < n_a:
            k_scale = dk_r ** -0.5
            tabs_p = _ret_rope_tables(pos_p, dk_r)
            tabs_s = _ret_rope_tables(pos_s, dk_r)
            q, q_s, h_p, h_s = lin(xp, xs, ret_wq, l, tn=512, out_dtype=BF16, norm=nrm_mix, epi="rope_ret",
                                   epi_p=tabs_p, epi_s=tabs_s, name="ret_q")
            k, k_s_ = lin(h_p, h_s, ret_wk, l, tn=1024, out_dtype=BF16, epi="rope_ret", epi_p=tabs_p,
                          epi_s=tabs_s, epi_scale=k_scale, name="ret_k")
            v, v_s_ = lin(h_p, h_s, ret_wv, l, tn=1024, out_dtype=BF16, name="ret_v")
            g, g_s = lin(h_p, h_s, ret_wg, l, tn=1024, out_dtype=BF16, name="ret_g")
            o, st_p = _retention_prompt(q, k, v, g, batch=bp, seq=lp, nh=nh_r, dk=dk_r, dv=dv_r,
                                        chunk=RET_CHUNK)
            o_s, st_s = _retention_decode(q_s, k_s_, v_s_, g_s, state_ret[l:l + 1], nh=nh_r, dk=dk_r,
                                          dv=dv_r)
            xp, xs = lin(o, o_s.reshape(bs, nh_r * dv_r), ret_wo, l, tn=512, out_dtype=F32, epi="residual",
                         epi_p=(xp, gm_p), epi_s=(xs, gm_s), name="ret_o")
            ret_p.append(st_p[0])
            ret_s.append(st_s[0])
        else:
            bi = l - n_a
            tabs_p = _diff_rope_tables(pos_p, dh_d, rope_dim)
            tabs_s = _diff_rope_tables(pos_s, dh_d, rope_dim)
            if bi == 0:
                (sh_p, sc_p), (sh_s, sc_s) = split_mod(
                    _modulation(c_all, kv_wmod[None], kv_bmod[None], 0), 2)
                nrm_kv = (kv_norm, sh_p, sc_p, sh_s, sc_s)
                k_p, k_s, h_p, h_s = lin(xp, xs, kv_wk[None], 0, tn=512, out_dtype=F32, norm=nrm_kv,
                                         epi="rope_diff", epi_p=tabs_p, epi_s=tabs_s, name="kv_k")
                v_p, v_s = lin(h_p, h_s, kv_wv[None], 0, tn=1024, out_dtype=F32, name="kv_v")
            lam_init = 0.8 - 0.6 * math.exp(-0.3 * l)
            lam = (jnp.exp(jnp.sum(diff_lq1[bi].astype(F32) * diff_lk1[bi].astype(F32)))
                   - jnp.exp(jnp.sum(diff_lq2[bi].astype(F32) * diff_lk2[bi].astype(F32))) + lam_init)
            lam_sc = jnp.stack([lam, jnp.float32(1.0 - lam_init)])
            q_scale = dh_d ** -0.5
            q, q_s, _, _ = lin(xp, xs, diff_wq, bi, tn=512, out_dtype=BF16, norm=nrm_mix, epi="rope_diff",
                               epi_p=tabs_p, epi_s=tabs_s, epi_scale=q_scale, name="diff_q")
            o = _diff_attention_prompt(q, k_p, v_p, lam_sc, diff_subln[bi], batch=bp, seq=lp, nh=nh_d,
                                       dh=dh_d, dv=dv_d)
            o_s = _diff_attention_decode(q_s, k_s, v_s, cache_k, cache_v, page_table, lam_sc,
                                         diff_subln[bi], nh=nh_d, dh=dh_d, dv=dv_d)
            xp, xs = lin(o, o_s.reshape(bs, nh_d * dv_d), diff_wo, bi, tn=512, out_dtype=F32,
                         epi="residual", epi_p=(xp, gm_p), epi_s=(xs, gm_s), name="diff_o")

        wa, wg_ = _narrow(ffn_wup, l, 2, rows=256)
        cw, cb = ffn_conv[l], ffn_conv_b[l].reshape(1, d_ff)
        buf = jnp.swapaxes(state_conv[l], 0, 1)
        u, tails, a_s, u_s = _ffn_up(xp, xs, (norm_ffn[l], sf_p, cf_p, sf_s, cf_s), wa, wg_, buf, cw, cb,
                                     tm=tm, tn=512, rows_per_group=lp, name="ffn_up")
        (wdn,) = _narrow(ffn_wdown, l, 1, rows=d_ff // 8)
        xp, xs = lin(u, u_s, wdn[None], 0, tn=512, out_dtype=F32, epi="residual",
                     epi_p=(xp, gf_p), epi_s=(xs, gf_s), name="ffn_down")
        last = tails.reshape(bp, lp // tm, SUBLANES, d_ff)[:, -1, SUBLANES - (CONV_W - 1):, :]
        conv_p.append(last)
        conv_s.append(jnp.concatenate([state_conv[l][:, 1:], a_s[:, None, :]], axis=1))

    y_prompt = _final_norm(xp, norm_final, tm).reshape(bp, lp, d)
    y_sample = _final_norm(xs, norm_final, bs).reshape(bs, ls, d)
    return (y_prompt, y_sample, jnp.stack(ret_p), jnp.stack(ret_s), jnp.stack(conv_p), jnp.stack(conv_s),
            k_p.reshape(bp, lp, nh_d, 2, dh_d), v_p.reshape(bp, lp, nh_d, dv_d),
            k_s.reshape(bs, ls, nh_d, 2, dh_d), v_s.reshape(bs, ls, nh_d, dv_d))
```

```python
import functools
import math

import jax
import jax.numpy as jnp
from jax import lax
from jax.experimental import pallas as pl
from jax.experimental.pallas import tpu as pltpu

F32 = jnp.float32
BF16 = jnp.bfloat16

NORM_EPS = 1e-6
RET_THETA = 10000.0
ROPE_THETA = 500000.0
RET_CHUNK = 256
CONV_W = 3

V7X_VMEM_LIMIT_BYTES = 56 * 1024 * 1024
LANES = 128
SUBLANES = 8


def _params(semantics):
    return pltpu.CompilerParams(dimension_semantics=semantics,
                                vmem_limit_bytes=V7X_VMEM_LIMIT_BYTES)


def _silu(x):
    return x * jax.nn.sigmoid(x)


def _rms(x):
    return x * lax.rsqrt(jnp.mean(x * x, axis=-1, keepdims=True) + NORM_EPS)


def _dot_nt(a, b):
    return lax.dot_general(a, b, (((1,), (1,)), ((), ())), preferred_element_type=F32)


def _dot_tn(a, b):
    return lax.dot_general(a, b, (((0,), (0,)), ((), ())), preferred_element_type=F32)


def _mod_kernel(c_ref, w_ref, b_ref, o_ref, s_ref):
    @pl.when(pl.program_id(0) == 0)
    def _():
        s_ref[...] = _silu(c_ref[...]).astype(BF16)

    acc = jnp.dot(s_ref[...], w_ref[...].astype(BF16), preferred_element_type=F32)
    o_ref[...] = acc + b_ref[...]


def _modulation(c, w, b, layer, tn=1024):
    m, d = c.shape
    n = w.shape[2]
    return pl.pallas_call(
        _mod_kernel,
        out_shape=jax.ShapeDtypeStruct((m, n), F32),
        grid=(n // tn,),
        in_specs=[pl.BlockSpec((m, d), lambda j: (0, 0)),
                  pl.BlockSpec((None, d, tn), lambda j: (layer, 0, j)),
                  pl.BlockSpec((None, 1, tn), lambda j: (layer, 0, j))],
        out_specs=pl.BlockSpec((m, tn), lambda j: (0, j)),
        scratch_shapes=[pltpu.VMEM((m, d), BF16)],
        compiler_params=_params(("arbitrary",)),
        name="modulation",
    )(c, w, b.reshape(b.shape[0], 1, n))


NORM_ROWS = 128


def _ada_norm_to(h_ref, x_ref, gain_ref, shift_ref, scale_ref):
    tm = x_ref.shape[0]
    if shift_ref.shape[1] != 1:
        y = _rms(x_ref[...].astype(F32)) * gain_ref[...]
        h_ref[...] = (y * (1.0 + scale_ref[0]) + shift_ref[0]).astype(BF16)
        return

    def body(r, carry):
        rows = pl.ds(pl.multiple_of(r * NORM_ROWS, NORM_ROWS), NORM_ROWS)
        y = _rms(x_ref[rows, :].astype(F32)) * gain_ref[...]
        h_ref[rows, :] = (y * (1.0 + scale_ref[0]) + shift_ref[0]).astype(BF16)
        return carry

    lax.fori_loop(0, tm // NORM_ROWS, body, 0)


def _epilogue(o_ref, acc, epi, epi_scale, extra):
    tn = acc.shape[1]
    if epi is None:
        o_ref[...] = acc.astype(o_ref.dtype)
    elif epi == "rope_ret":
        cos, sin = extra[0][...], extra[1][...]
        for c in range(tn // (2 * LANES)):
            lo, mid, hi = 2 * c * LANES, (2 * c + 1) * LANES, (2 * c + 2) * LANES
            x1, x2 = acc[:, lo:mid], acc[:, mid:hi]
            o_ref[:, lo:mid] = ((x1 * cos - x2 * sin) * epi_scale).astype(o_ref.dtype)
            o_ref[:, mid:hi] = ((x2 * cos + x1 * sin) * epi_scale).astype(o_ref.dtype)
    elif epi == "rope_diff":
        cc, sa, sb = extra[0][...], extra[1][...], extra[2][...]
        for c in range(tn // LANES):
            xc = acc[:, c * LANES:(c + 1) * LANES]
            r = (xc * cc + pltpu.roll(xc, 16, 1) * sa
                 + pltpu.roll(xc, LANES - 16, 1) * sb)
            o_ref[:, c * LANES:(c + 1) * LANES] = (r * epi_scale).astype(o_ref.dtype)
    elif epi == "residual":
        res_ref, gate_ref = extra
        o_ref[...] = (res_ref[...] + gate_ref[0] * acc).astype(o_ref.dtype)


def _take(it, n):
    return [next(it) for _ in range(n)]


def _linear_kernel(*refs, has_norm, epi, epi_scale):
    n_extra = {None: 0, "rope_ret": 2, "rope_diff": 3, "residual": 2}[epi]
    it = iter(refs)
    xp_ref, xs_ref = next(it), next(it)
    if has_norm:
        gain_ref, shp_ref, scp_ref, shs_ref, scs_ref = _take(it, 5)
    w_ref = next(it)
    extra_p, extra_s = _take(it, n_extra), _take(it, n_extra)
    op_ref, os_ref = next(it), next(it)
    i, j = pl.program_id(0), pl.program_id(1)

    if has_norm:
        hp_ref, hs_ref = next(it), next(it)

        @pl.when(j == 0)
        def _():
            _ada_norm_to(hp_ref, xp_ref, gain_ref, shp_ref, scp_ref)

        @pl.when((i == 0) & (j == 0))
        def _():
            _ada_norm_to(hs_ref, xs_ref, gain_ref, shs_ref, scs_ref)

        a = hp_ref[...]
    else:
        a = xp_ref[...].astype(BF16)

    w = w_ref[...].astype(BF16)
    _epilogue(op_ref, jnp.dot(a, w, preferred_element_type=F32), epi, epi_scale, extra_p)

    @pl.when(i == 0)
    def _():
        a_s = hs_ref[...] if has_norm else xs_ref[...].astype(BF16)
        _epilogue(os_ref.at[0], jnp.dot(a_s, w, preferred_element_type=F32), epi, epi_scale,
                  extra_s)

    @pl.when(i != 0)
    def _():
        os_ref[...] = jnp.zeros_like(os_ref)


def _group_spec(arr, tm, rows_per_group, width, col_fn):
    g, r, _ = arr.shape
    if r == 1:
        return pl.BlockSpec((1, 1, width),
                            lambda i, j: ((i * tm) // rows_per_group, 0, col_fn(j)))
    return pl.BlockSpec((1, r, width), lambda i, j: (0, 0, col_fn(j)))


def _table_spec(tab, tm, rows_per_group):
    if tab.shape[0] == 1:
        return pl.BlockSpec((1, LANES), lambda i, j: (0, 0))
    nblk = rows_per_group // tm
    return pl.BlockSpec((tm, LANES), lambda i, j: (i % nblk, 0))


def _decode_out_spec(ms, tn):
    return pl.BlockSpec((1, ms, tn), lambda i, j: (i, 0, j))


def _linear(xp, xs, w, layer, *, tm, tn, out_dtype, rows_per_group, norm=None, epi=None,
            epi_p=(), epi_s=(), epi_scale=1.0, name="linear"):
    m, k = xp.shape
    ms = xs.shape[0]
    n = w.shape[2]
    has_norm = norm is not None
    args = [xp, xs]
    specs = [pl.BlockSpec((tm, k), lambda i, j: (i, 0)),
             pl.BlockSpec((ms, k), lambda i, j: (0, 0))]
    if has_norm:
        gain, shift_p, scale_p, shift_s, scale_s = norm
        args += [gain.reshape(1, k), shift_p, scale_p, shift_s, scale_s]
        specs += [pl.BlockSpec((1, k), lambda i, j: (0, 0)),
                  _group_spec(shift_p, tm, rows_per_group, k, lambda j: 0),
                  _group_spec(scale_p, tm, rows_per_group, k, lambda j: 0),
                  _group_spec(shift_s, tm, rows_per_group, k, lambda j: 0),
                  _group_spec(scale_s, tm, rows_per_group, k, lambda j: 0)]
    args.append(w)
    specs.append(pl.BlockSpec((None, k, tn), lambda i, j: (layer, 0, j)))
    if epi in ("rope_ret", "rope_diff"):
        for tab in tuple(epi_p) + tuple(epi_s):
            args.append(tab)
            specs.append(_table_spec(tab, tm, rows_per_group))
    elif epi == "residual":
        (res_p, gate_p), (res_s, gate_s) = epi_p, epi_s
        args += [res_p, gate_p, res_s, gate_s]
        specs += [pl.BlockSpec((tm, tn), lambda i, j: (i, j)),
                  _group_spec(gate_p, tm, rows_per_group, tn, lambda j: j),
                  pl.BlockSpec((ms, tn), lambda i, j: (0, j)),
                  _group_spec(gate_s, tm, rows_per_group, tn, lambda j: j)]
    out_shape = [jax.ShapeDtypeStruct((m, n), out_dtype),
                 jax.ShapeDtypeStruct((m // tm, ms, n), F32)]
    out_specs = [pl.BlockSpec((tm, tn), lambda i, j: (i, j)), _decode_out_spec(ms, tn)]
    if has_norm:
        out_shape += [jax.ShapeDtypeStruct((m, k), BF16), jax.ShapeDtypeStruct((ms, k), BF16)]
        out_specs += [pl.BlockSpec((tm, k), lambda i, j: (i, 0)),
                      pl.BlockSpec((ms, k), lambda i, j: (0, 0))]
    kern = functools.partial(_linear_kernel, has_norm=has_norm, epi=epi, epi_scale=epi_scale)
    outs = pl.pallas_call(
        kern,
        out_shape=out_shape,
        grid=(m // tm, n // tn),
        in_specs=specs,
        out_specs=out_specs,
        compiler_params=_params(("arbitrary", "arbitrary")),
        name=name,
    )(*args)
    if has_norm:
        return outs[0], outs[1][0], outs[2], outs[3]
    return outs[0], outs[1][0]


def _ada_norm_kernel(xp_ref, xs_ref, gain_ref, shp_ref, scp_ref, shs_ref, scs_ref, hp_ref, hs_ref):
    _ada_norm_to(hp_ref, xp_ref, gain_ref, shp_ref, scp_ref)

    @pl.when(pl.program_id(0) == 0)
    def _():
        _ada_norm_to(hs_ref, xs_ref, gain_ref, shs_ref, scs_ref)


def _ada_norm(xp, xs, norm, *, tm, rows_per_group):
    m, k = xp.shape
    ms = xs.shape[0]
    gain, shift_p, scale_p, shift_s, scale_s = norm
    return pl.pallas_call(
        _ada_norm_kernel,
        out_shape=[jax.ShapeDtypeStruct((m, k), BF16), jax.ShapeDtypeStruct((ms, k), BF16)],
        grid=(m // tm, 1),
        in_specs=[pl.BlockSpec((tm, k), lambda i, j: (i, 0)),
                  pl.BlockSpec((ms, k), lambda i, j: (0, 0)),
                  pl.BlockSpec((1, k), lambda i, j: (0, 0)),
                  _group_spec(shift_p, tm, rows_per_group, k, lambda j: 0),
                  _group_spec(scale_p, tm, rows_per_group, k, lambda j: 0),
                  _group_spec(shift_s, tm, rows_per_group, k, lambda j: 0),
                  _group_spec(scale_s, tm, rows_per_group, k, lambda j: 0)],
        out_specs=[pl.BlockSpec((tm, k), lambda i, j: (i, 0)),
                   pl.BlockSpec((ms, k), lambda i, j: (0, 0))],
        compiler_params=_params(("arbitrary", "arbitrary")),
        name="ada_norm",
    )(xp, xs, gain.reshape(1, k), shift_p, scale_p, shift_s, scale_s)


def _ffn_up_kernel(xp_ref, xs_ref, gain_ref, shp_ref, scp_ref, shs_ref, scs_ref, wa_ref, wg_ref,
                   buf_ref, cw_ref, cb_ref, u_ref, tail_ref, as_ref, us_ref,
                   hp_ref, hs_ref, carry_ref, *, tiles_per_seq):
    i, j = pl.program_id(0), pl.program_id(1)

    @pl.when(j == 0)
    def _():
        _ada_norm_to(hp_ref, xp_ref, gain_ref, shp_ref, scp_ref)

    @pl.when((i == 0) & (j == 0))
    def _():
        _ada_norm_to(hs_ref, xs_ref, gain_ref, shs_ref, scs_ref)

    @pl.when(i % tiles_per_seq == 0)
    def _():
        carry_ref[j] = jnp.zeros(carry_ref.shape[1:], F32)

    wa, wg = wa_ref[...], wg_ref[...]
    h = hp_ref[...]
    a = jnp.dot(h, wa, preferred_element_type=F32)
    g = jnp.dot(h, wg, preferred_element_type=F32)
    tm, tn = a.shape
    halo = carry_ref[j]
    last = a[tm - SUBLANES:, :]
    carry_ref[j] = last
    tail_ref[0] = last
    row = lax.broadcasted_iota(jnp.int32, (tm, LANES), 0)
    for c in range(tn // LANES):
        cols = slice(c * LANES, (c + 1) * LANES)
        ac = a[:, cols]
        prev1 = halo[SUBLANES - 1:SUBLANES, cols]
        prev2 = halo[SUBLANES - 2:SUBLANES - 1, cols]
        a_m1 = jnp.where(row == 0, prev1, pltpu.roll(ac, 1, 0))
        a_m2 = jnp.where(row == 0, prev2, jnp.where(row == 1, prev1, pltpu.roll(ac, 2, 0)))
        conv = (cb_ref[:, cols] + a_m2 * cw_ref[0:1, cols] + a_m1 * cw_ref[1:2, cols]
                + ac * cw_ref[2:3, cols])
        u_ref[:, cols] = (_silu(conv) * g[:, cols]).astype(BF16)

    @pl.when(i == 0)
    def _():
        hs = hs_ref[...]
        a_s = jnp.dot(hs, wa, preferred_element_type=F32)
        g_s = jnp.dot(hs, wg, preferred_element_type=F32)
        conv = (cb_ref[...] + buf_ref[0] * cw_ref[0:1, :] + buf_ref[1] * cw_ref[1:2, :]
                + a_s * cw_ref[2:3, :])
        as_ref[0] = a_s
        us_ref[0] = (_silu(conv) * g_s).astype(BF16)

    @pl.when(i != 0)
    def _():
        as_ref[...] = jnp.zeros_like(as_ref)
        us_ref[...] = jnp.zeros_like(us_ref)


def _ffn_up(xp, xs, norm, wa, wg, buf, cw, cb, *, tm, tn, rows_per_group, name="ffn_up"):
    m, k = xp.shape
    ms = xs.shape[0]
    f = wg.shape[1]
    gain, shift_p, scale_p, shift_s, scale_s = norm
    args = [xp, xs, gain.reshape(1, k), shift_p, scale_p, shift_s, scale_s, wa, wg, buf, cw, cb]
    specs = [pl.BlockSpec((tm, k), lambda i, j: (i, 0)),
             pl.BlockSpec((ms, k), lambda i, j: (0, 0)),
             pl.BlockSpec((1, k), lambda i, j: (0, 0)),
             _group_spec(shift_p, tm, rows_per_group, k, lambda j: 0),
             _group_spec(scale_p, tm, rows_per_group, k, lambda j: 0),
             _group_spec(shift_s, tm, rows_per_group, k, lambda j: 0),
             _group_spec(scale_s, tm, rows_per_group, k, lambda j: 0),
             pl.BlockSpec((k, tn), lambda i, j: (0, j)),
             pl.BlockSpec((k, tn), lambda i, j: (0, j)),
             pl.BlockSpec((CONV_W - 1, ms, tn), lambda i, j: (0, 0, j)),
             pl.BlockSpec((CONV_W, tn), lambda i, j: (0, j)),
             pl.BlockSpec((1, tn), lambda i, j: (0, j))]
    n_col = pl.cdiv(f, tn)
    out_shape = [jax.ShapeDtypeStruct((m, f), BF16),
                 jax.ShapeDtypeStruct((m // tm, SUBLANES, f), F32),
                 jax.ShapeDtypeStruct((m // tm, ms, f), F32),
                 jax.ShapeDtypeStruct((m // tm, ms, f), BF16)]
    out_specs = [pl.BlockSpec((tm, tn), lambda i, j: (i, j)),
                 pl.BlockSpec((1, SUBLANES, tn), lambda i, j: (i, 0, j)),
                 _decode_out_spec(ms, tn), _decode_out_spec(ms, tn)]
    u, tails, a_s, u_s = pl.pallas_call(
        functools.partial(_ffn_up_kernel, tiles_per_seq=rows_per_group // tm),
        out_shape=out_shape,
        grid=(m // tm, n_col),
        in_specs=specs,
        out_specs=out_specs,
        scratch_shapes=[pltpu.VMEM((tm, k), BF16), pltpu.VMEM((ms, k), BF16),
                        pltpu.VMEM((n_col, SUBLANES, tn), F32)],
        compiler_params=_params(("arbitrary", "arbitrary")),
        name=name,
    )(*args)
    return u, tails, a_s[0], u_s[0]


def _narrow_kernel(w_ref, *o_refs):
    width = o_refs[0].shape[1]
    for t, o_ref in enumerate(o_refs):
        o_ref[...] = w_ref[:, t * width:(t + 1) * width].astype(BF16)


def _narrow(w, layer, parts, rows):
    _, k, n = w.shape
    width = n // parts
    return pl.pallas_call(
        _narrow_kernel,
        out_shape=[jax.ShapeDtypeStruct((k, width), BF16)] * parts,
        grid=(k // rows,),
        in_specs=[pl.BlockSpec((None, rows, n), lambda r: (layer, r, 0))],
        out_specs=[pl.BlockSpec((rows, width), lambda r: (r, 0))] * parts,
        compiler_params=_params(("parallel",)),
        name="narrow_weights",
    )(w)


def _ret_prompt_kernel(sdec_ref, q_ref, k_ref, v_ref, g_ref, decay_ref, qdec_ref,
                       kdec_ref, o_ref, st_ref, *, chunk):
    sdec = sdec_ref[pl.program_id(1)]
    decay, qdec, kdec = decay_ref[0], qdec_ref[0], kdec_ref[0]
    st_ref[...] = jnp.zeros_like(st_ref)
    for n in range(q_ref.shape[0] // chunk):
        rows = slice(n * chunk, (n + 1) * chunk)
        s = st_ref[0, 0, 0]
        q, k, v = q_ref[rows, :], k_ref[rows, :], v_ref[rows, :]
        att = _dot_nt(q, k) * decay
        o = (jnp.dot(att.astype(BF16), v, preferred_element_type=F32)
             + jnp.dot(q, s.astype(BF16), preferred_element_type=F32) * qdec)
        kd = (k.astype(F32) * kdec).astype(BF16)
        st_ref[0, 0, 0] = s * sdec + _dot_tn(kd, v)
        o_ref[rows, :] = (_rms(o) * _silu(g_ref[rows, :].astype(F32))).astype(BF16)


def _ret_tables(chunk, nh):
    log_g = jnp.log1p(-jnp.exp2(-5.0 - jnp.arange(nh, dtype=F32)))
    idx = jnp.arange(chunk, dtype=F32)
    rel = idx[:, None] - idx[None, :]
    decay = jnp.where(rel >= 0, jnp.exp(jnp.maximum(rel, 0.0)[None] * log_g[:, None, None]), 0.0)
    q_dec = jnp.exp((idx[None, :] + 1.0) * log_g[:, None])[:, :, None]
    k_dec = jnp.exp((chunk - 1.0 - idx)[None, :] * log_g[:, None])[:, :, None]
    s_dec = jnp.exp(chunk * log_g)
    return decay, q_dec, k_dec, s_dec


def _retention_prompt(q, k, v, g, *, batch, seq, nh, dk, dv, chunk):
    m = q.shape[0]
    decay, q_dec, k_dec, s_dec = _ret_tables(chunk, nh)
    seq_head = lambda b, h: (b, h)
    head = lambda b, h: (h, 0, 0)
    return pl.pallas_call(
        functools.partial(_ret_prompt_kernel, chunk=chunk),
        out_shape=[jax.ShapeDtypeStruct((m, nh * dv), BF16),
                   jax.ShapeDtypeStruct((1, batch, nh, dk, dv), F32)],
        grid=(batch, nh),
        in_specs=[pl.BlockSpec(memory_space=pltpu.SMEM),
                  pl.BlockSpec((seq, dk), seq_head),
                  pl.BlockSpec((seq, dk), seq_head),
                  pl.BlockSpec((seq, dv), seq_head),
                  pl.BlockSpec((seq, dv), seq_head),
                  pl.BlockSpec((1, chunk, chunk), head),
                  pl.BlockSpec((1, chunk, 1), head),
                  pl.BlockSpec((1, chunk, 1), head)],
        out_specs=[pl.BlockSpec((seq, dv), seq_head),
                   pl.BlockSpec((1, 1, 1, dk, dv), lambda b, h: (0, b, h, 0, 0))],
        compiler_params=_params(("parallel", "parallel")),
        name="retention_prompt",
    )(s_dec, q, k, v, g, decay, q_dec, k_dec)


RET_DECODE_SEQS = 2


def _ret_decode_kernel(q_ref, k_ref, v_ref, g_ref, st_ref, dec_ref, o_ref, sto_ref, *, nh):
    decay, qdec, kdec, sdec = (dec_ref[:, i:i + 1] for i in range(4))
    rows = lax.broadcasted_iota(jnp.int32, (nh, 1), 0)
    for b in range(RET_DECODE_SEQS):
        q, k, v = q_ref[b], k_ref[b], v_ref[b]
        att = jnp.sum(q * k, axis=-1, keepdims=True) * decay
        qb, vb = q.astype(BF16), v.astype(BF16)
        kd = k * kdec
        qs = jnp.zeros(v.shape, F32)
        for h in range(nh):
            s = st_ref[0, b, h]
            qs = jnp.where(rows == h, jnp.dot(qb, s.astype(BF16), preferred_element_type=F32), qs)
            k_h = jnp.where(rows == h, kd, 0.0).astype(BF16)
            sto_ref[0, b, h] = s * sdec[h:h + 1, :] + _dot_tn(k_h, vb)
        o = att * v + qs * qdec
        o_ref[b] = _rms(o) * _silu(g_ref[b])


def _retention_decode(q, k, v, g, state, *, nh, dk, dv):
    b = q.shape[0]
    decay, q_dec, k_dec, s_dec = _ret_tables(1, nh)
    dec = jnp.stack([decay[:, 0, 0], q_dec[:, 0, 0], k_dec[:, 0, 0], s_dec], axis=1)
    tok = lambda w: pl.BlockSpec((RET_DECODE_SEQS, nh, w), lambda i: (i, 0, 0))
    st = pl.BlockSpec((1, RET_DECODE_SEQS, nh, dk, dv), lambda i: (0, i, 0, 0, 0))
    return pl.pallas_call(
        functools.partial(_ret_decode_kernel, nh=nh),
        out_shape=[jax.ShapeDtypeStruct((b, nh, dv), F32),
                   jax.ShapeDtypeStruct(state.shape, F32)],
        grid=(b // RET_DECODE_SEQS,),
        in_specs=[tok(dk), tok(dk), tok(dv), tok(dv), st,
                  pl.BlockSpec((nh, 4), lambda i: (0, 0))],
        out_specs=[tok(dv), st],
        compiler_params=_params(("parallel",)),
        name="retention_decode",
    )(q.reshape(b, nh, dk), k.reshape(b, nh, dk), v.reshape(b, nh, dv),
      g.reshape(b, nh, dv), state, dec)


def _diff_out(o, subln, out_scale):
    return _rms(o) * subln * out_scale


def _lane_tile_reduce(x, op):
    out = x[:, :LANES]
    for j in range(1, x.shape[1] // LANES):
        out = op(out, x[:, j * LANES:(j + 1) * LANES])
    return out


def _diff_prompt_kernel(lam_ref, q_ref, k_ref, v_ref, subln_ref, o_ref,
                        kb_ref, vb_ref, s_ref, p_ref, *, tq, dh):
    kb_ref[...] = k_ref[...].astype(BF16)
    vb_ref[...] = v_ref[...].astype(BF16)
    row = lax.broadcasted_iota(jnp.int32, (tq, tq), 0)
    col = lax.broadcasted_iota(jnp.int32, (tq, tq), 1)
    for qi in range(q_ref.shape[0] // tq):
        rows = slice(qi * tq, (qi + 1) * tq)
        kv_len = (qi + 1) * tq
        outs = []
        for c in range(2):
            comp = slice(c * dh, (c + 1) * dh)
            q = q_ref[rows, comp]
            m_part = None
            for ki in range(qi + 1):
                keys = slice(ki * tq, (ki + 1) * tq)
                s = _dot_nt(q, kb_ref[keys, comp])
                if ki == qi:
                    s = jnp.where(col <= row, s, -jnp.inf)
                s_ref[c, :, keys] = s
                blk_max = _lane_tile_reduce(s, jnp.maximum)
                m_part = blk_max if m_part is None else jnp.maximum(m_part, blk_max)
            m = jnp.max(m_part, axis=-1, keepdims=True)
            l_part = jnp.zeros((tq, LANES), F32)
            for ki in range(qi + 1):
                keys = slice(ki * tq, (ki + 1) * tq)
                p = jnp.exp(s_ref[c, :, keys] - m)
                l_part = l_part + _lane_tile_reduce(p, jnp.add)
                p_ref[c, :, keys] = p.astype(BF16)
            l = jnp.sum(l_part, axis=-1, keepdims=True)
            pv = jnp.dot(p_ref[c, :, :kv_len], vb_ref[:kv_len, :], preferred_element_type=F32)
            outs.append(pv / l)
        o = outs[0] - lam_ref[0] * outs[1]
        o_ref[rows, :] = _diff_out(o, subln_ref[...], lam_ref[1]).astype(BF16)


def _diff_attention_prompt(q, k, v, lam, subln, *, batch, seq, nh, dh, dv, tq=256):
    m = q.shape[0]
    seq_head = lambda b, h: (b, h)
    return pl.pallas_call(
        functools.partial(_diff_prompt_kernel, tq=tq, dh=dh),
        out_shape=jax.ShapeDtypeStruct((m, nh * dv), BF16),
        grid=(batch, nh),
        in_specs=[pl.BlockSpec(memory_space=pltpu.SMEM),
                  pl.BlockSpec((seq, 2 * dh), seq_head),
                  pl.BlockSpec((seq, 2 * dh), seq_head),
                  pl.BlockSpec((seq, dv), seq_head),
                  pl.BlockSpec((1, dv), lambda b, h: (0, 0))],
        out_specs=pl.BlockSpec((seq, dv), seq_head),
        scratch_shapes=[pltpu.VMEM((seq, 2 * dh), BF16), pltpu.VMEM((seq, dv), BF16),
                        pltpu.VMEM((2, tq, seq), F32), pltpu.VMEM((2, tq, seq), BF16)],
        compiler_params=_params(("parallel", "parallel")),
        name="diff_attention_prompt",
    )(lam, q, k, v, subln.reshape(1, dv))


PAGES_PER_STEP = 8


def _diff_decode_kernel(pt_ref, lam_ref, q_ref, kn_ref, vn_ref, subln_ref, *rest, nh, page):
    del pt_ref
    ck_refs, cv_refs = rest[:PAGES_PER_STEP], rest[PAGES_PER_STEP:2 * PAGES_PER_STEP]
    o_ref, m_ref, l_ref, acc_ref = rest[2 * PAGES_PER_STEP:]
    p = pl.program_id(1)
    rows_kv = page * nh

    @pl.when(p == 0)
    def _():
        m_ref[...] = jnp.full_like(m_ref, -jnp.inf)
        l_ref[...] = jnp.zeros_like(l_ref)
        acc_ref[...] = jnp.zeros_like(acc_ref)

    q = q_ref[0]
    qb = q.astype(BF16)
    row = lax.broadcasted_iota(jnp.int32, (2 * nh, rows_kv), 0)
    col = lax.broadcasted_iota(jnp.int32, (2 * nh, rows_kv), 1)
    own_head = (col % nh) == (row % nh)
    scores = []
    for ck_ref in ck_refs:
        k0 = ck_ref[0, pl.ds(0, rows_kv, stride=2), :].astype(BF16)
        k1 = ck_ref[0, pl.ds(1, rows_kv, stride=2), :].astype(BF16)
        s = jnp.concatenate([_dot_nt(qb[:nh], k0), _dot_nt(qb[nh:], k1)], axis=0)
        scores.append(jnp.where(own_head, s, -jnp.inf))
    m_old = m_ref[...]
    m_new = jnp.maximum(m_old, jnp.max(functools.reduce(jnp.maximum, scores), axis=-1, keepdims=True))
    alpha = jnp.exp(m_old - m_new)
    l = alpha * l_ref[...]
    acc = alpha * acc_ref[...]
    for s, cv_ref in zip(scores, cv_refs):
        pr = jnp.exp(s - m_new)
        l = l + jnp.sum(pr, axis=-1, keepdims=True)
        acc = acc + jnp.dot(pr.astype(BF16), cv_ref[0].astype(BF16), preferred_element_type=F32)
    m_ref[...] = m_new
    l_ref[...] = l
    acc_ref[...] = acc

    @pl.when(p == pl.num_programs(1) - 1)
    def _():
        s_new = jnp.sum(q * kn_ref[0], axis=-1, keepdims=True)
        m_old = m_ref[...]
        m_new = jnp.maximum(m_old, s_new)
        alpha = jnp.exp(m_old - m_new)
        pn = jnp.exp(s_new - m_new)
        l = alpha * l_ref[...] + pn
        vn = vn_ref[0]
        acc = alpha * acc_ref[...] + pn * jnp.concatenate([vn, vn], axis=0)
        on = acc / l
        o = on[:nh] - lam_ref[0] * on[nh:]
        o_ref[0] = _diff_out(o, subln_ref[...], lam_ref[1])


def _diff_attention_decode(q, k_new, v_new, cache_k, cache_v, page_table, lam, subln,
                           *, nh, dh, dv):
    b = q.shape[0]
    n_pool, page = cache_k.shape[0], cache_k.shape[1]
    n_pages = page_table.shape[1]
    ck = cache_k.reshape(n_pool, page * nh * 2, dh)
    cv = cache_v.reshape(n_pool, page * nh, dv)
    comp_major = lambda t: t.reshape(b, nh, 2, dh).transpose(0, 2, 1, 3).reshape(b, 2 * nh, dh)
    tok = lambda r, w: pl.BlockSpec((1, r, w), lambda i, p, pt: (i, 0, 0))

    def page_spec(rows, width, r):
        return pl.BlockSpec((1, rows, width),
                            lambda i, p, pt: (pt[i * n_pages + p * PAGES_PER_STEP + r], 0, 0))

    steps = n_pages // PAGES_PER_STEP
    grid_spec = pltpu.PrefetchScalarGridSpec(
        num_scalar_prefetch=1,
        grid=(b, steps),
        in_specs=([pl.BlockSpec(memory_space=pltpu.SMEM),
                   tok(2 * nh, dh), tok(2 * nh, dh), tok(nh, dv),
                   pl.BlockSpec((1, dv), lambda i, p, pt: (0, 0))]
                  + [page_spec(page * nh * 2, dh, r) for r in range(PAGES_PER_STEP)]
                  + [page_spec(page * nh, dv, r) for r in range(PAGES_PER_STEP)]),
        out_specs=tok(nh, dv),
        scratch_shapes=[pltpu.VMEM((2 * nh, 1), F32), pltpu.VMEM((2 * nh, 1), F32),
                        pltpu.VMEM((2 * nh, dv), F32)],
    )
    return pl.pallas_call(
        functools.partial(_diff_decode_kernel, nh=nh, page=page),
        out_shape=jax.ShapeDtypeStruct((b, nh, dv), F32),
        grid_spec=grid_spec,
        compiler_params=_params(("parallel", "arbitrary")),
        name="diff_attention_decode",
    )(page_table.reshape(-1), lam, comp_major(q), comp_major(k_new),
      v_new.reshape(b, nh, dv), subln.reshape(1, dv),
      *([ck] * PAGES_PER_STEP), *([cv] * PAGES_PER_STEP))


def _final_norm_kernel(x_ref, gain_ref, o_ref):
    o_ref[...] = _rms(x_ref[...]) * gain_ref[...]


def _final_norm(x, gain, tm):
    m, d = x.shape
    return pl.pallas_call(
        _final_norm_kernel,
        out_shape=jax.ShapeDtypeStruct((m, d), F32),
        grid=(m // tm,),
        in_specs=[pl.BlockSpec((tm, d), lambda i: (i, 0)),
                  pl.BlockSpec((1, d), lambda i: (0, 0))],
        out_specs=pl.BlockSpec((tm, d), lambda i: (i, 0)),
        compiler_params=_params(("parallel",)),
        name="final_norm",
    )(x, gain.reshape(1, d))


def _ret_rope_tables(pos, dk):
    half = dk // 2
    inv = jnp.float32(RET_THETA) ** (-jnp.arange(half, dtype=F32) / half)
    ang = pos.astype(F32)[:, None] * inv[None, :]
    return jnp.cos(ang), jnp.sin(ang)


def _diff_rope_tables(pos, dh, rope_dim):
    half = rope_dim // 2
    inv = jnp.float32(ROPE_THETA) ** (-jnp.arange(half, dtype=F32) / half)
    ang = pos.astype(F32)[:, None] * inv[None, :]
    cos, sin = jnp.cos(ang), jnp.sin(ang)
    n = pos.shape[0]
    rest = dh - rope_dim
    coef = jnp.concatenate([cos, cos, jnp.ones((n, rest), F32)], axis=1)
    from_lo = jnp.concatenate([jnp.zeros((n, half), F32), sin, jnp.zeros((n, rest), F32)], axis=1)
    from_hi = jnp.concatenate([-sin, jnp.zeros((n, half + rest), F32)], axis=1)
    return coef, from_lo, from_hi


def kernel(x_prompt, x_sample, state_ret, state_conv, cache_k, cache_v, page_table, c_prompt, c_sample, ret_wq, ret_wk, ret_wv, ret_wg, ret_wo, kv_norm, kv_wmod, kv_bmod, kv_wk, kv_wv, diff_wq, diff_lq1, diff_lk1, diff_lq2, diff_lk2, diff_subln, diff_wo, norm_mix, norm_ffn, w_mod, b_mod, ffn_wup, ffn_conv, ffn_conv_b, ffn_wdown, norm_final):
    bp, lp, d = x_prompt.shape
    bs, ls, _ = x_sample.shape
    assert ls == 1, "decode group carries one new token per sequence"
    depth = w_mod.shape[0]
    n_a = ret_wq.shape[0]
    nh_r, dk_r, dv_r = state_ret.shape[2], state_ret.shape[3], state_ret.shape[4]
    nh_d, dh_d, dv_d = cache_k.shape[2], cache_k.shape[4], cache_v.shape[3]
    rope_dim = dh_d // 4
    d_ff = ffn_conv.shape[2]
    past_len = page_table.shape[1] * cache_k.shape[1]
    mp = bp * lp
    tm = 1024

    pos_p = jnp.arange(lp)
    pos_s = past_len + jnp.arange(ls)

    xp = x_prompt.reshape(mp, d)
    xs = x_sample.reshape(bs, d)

    n_c = bp + bs
    n_c_pad = -(-n_c // SUBLANES) * SUBLANES
    c_all = jnp.concatenate([c_prompt, c_sample, jnp.zeros((n_c_pad - n_c, d), F32)], axis=0)

    def split_mod(mod, n):
        mod_p = mod[:bp].reshape(bp, 1, n, d)
        mod_s = mod[bp:n_c].reshape(1, bs, n, d)
        return ([mod_p[:, :, i] for i in range(n)], [mod_s[:, :, i] for i in range(n)])

    lin = functools.partial(_linear, tm=tm, rows_per_group=lp)

    ret_p, ret_s, conv_p, conv_s = [], [], [], []
    k_p = v_p = k_s = v_s = None
    for l in range(depth):
        (sm_p, cm_p, gm_p, sf_p, cf_p, gf_p), (sm_s, cm_s, gm_s, sf_s, cf_s, gf_s) = split_mod(
            _modulation(c_all, w_mod, b_mod, l), 6)
        nrm_mix = (norm_mix[l], sm_p, cm_p, sm_s, cm_s)
        if l < n_a:
            k_scale = dk_r ** -0.5
            tabs_p = _ret_rope_tables(pos_p, dk_r)
            tabs_s = _ret_rope_tables(pos_s, dk_r)
            h_p, h_s = _ada_norm(xp, xs, nrm_mix, tm=512, rows_per_group=lp)
            q, q_s = lin(h_p, h_s, ret_wq, l, tn=1024, out_dtype=BF16, epi="rope_ret", epi_p=tabs_p,
                         epi_s=tabs_s, name="ret_q")
            k, k_s_ = lin(h_p, h_s, ret_wk, l, tn=1024, out_dtype=BF16, epi="rope_ret", epi_p=tabs_p,
                          epi_s=tabs_s, epi_scale=k_scale, name="ret_k")
            v, v_s_ = lin(h_p, h_s, ret_wv, l, tn=1024, out_dtype=BF16, name="ret_v")
            g, g_s = lin(h_p, h_s, ret_wg, l, tn=1024, out_dtype=BF16, name="ret_g")
            o, st_p = _retention_prompt(q, k, v, g, batch=bp, seq=lp, nh=nh_r, dk=dk_r, dv=dv_r,
                                        chunk=RET_CHUNK)
            o_s, st_s = _retention_decode(q_s, k_s_, v_s_, g_s, state_ret[l:l + 1], nh=nh_r, dk=dk_r,
                                          dv=dv_r)
            xp, xs = lin(o, o_s.reshape(bs, nh_r * dv_r), ret_wo, l, tn=512, out_dtype=F32, epi="residual",
                         epi_p=(xp, gm_p), epi_s=(xs, gm_s), name="ret_o")
            ret_p.append(st_p[0])
            ret_s.append(st_s[0])
        else:
            bi = l - n_a
            tabs_p = _diff_rope_tables(pos_p, dh_d, rope_dim)
            tabs_s = _diff_rope_tables(pos_s, dh_d, rope_dim)
            if bi == 0:
                (sh_p, sc_p), (sh_s, sc_s) = split_mod(
                    _modulation(c_all, kv_wmod[None], kv_bmod[None], 0), 2)
                nrm_kv = (kv_norm, sh_p, sc_p, sh_s, sc_s)
                h_p, h_s = _ada_norm(xp, xs, nrm_kv, tm=512, rows_per_group=lp)
                k_p, k_s = lin(h_p, h_s, kv_wk[None], 0, tn=1024, out_dtype=F32, epi="rope_diff",
                               epi_p=tabs_p, epi_s=tabs_s, name="kv_k")
                v_p, v_s = lin(h_p, h_s, kv_wv[None], 0, tn=1024, out_dtype=F32, name="kv_v")
            lam_init = 0.8 - 0.6 * math.exp(-0.3 * l)
            lam = (jnp.exp(jnp.sum(diff_lq1[bi].astype(F32) * diff_lk1[bi].astype(F32)))
                   - jnp.exp(jnp.sum(diff_lq2[bi].astype(F32) * diff_lk2[bi].astype(F32))) + lam_init)
            lam_sc = jnp.stack([lam, jnp.float32(1.0 - lam_init)])
            q_scale = dh_d ** -0.5
            h_p, h_s = _ada_norm(xp, xs, nrm_mix, tm=512, rows_per_group=lp)
            q, q_s = lin(h_p, h_s, diff_wq, bi, tn=1024, out_dtype=BF16, epi="rope_diff", epi_p=tabs_p,
                         epi_s=tabs_s, epi_scale=q_scale, name="diff_q")
            o = _diff_attention_prompt(q, k_p, v_p, lam_sc, diff_subln[bi], batch=bp, seq=lp, nh=nh_d,
                                       dh=dh_d, dv=dv_d)
            o_s = _diff_attention_decode(q_s, k_s, v_s, cache_k, cache_v, page_table, lam_sc,
                                         diff_subln[bi], nh=nh_d, dh=dh_d, dv=dv_d)
            xp, xs = lin(o, o_s.reshape(bs, nh_d * dv_d), diff_wo, bi, tn=512, out_dtype=F32,
                         epi="residual", epi_p=(xp, gm_p), epi_s=(xs, gm_s), name="diff_o")

        wa, wg_ = _narrow(ffn_wup, l, 2, rows=256)
        cw, cb = ffn_conv[l], ffn_conv_b[l].reshape(1, d_ff)
        buf = jnp.swapaxes(state_conv[l], 0, 1)
        u, tails, a_s, u_s = _ffn_up(xp, xs, (norm_ffn[l], sf_p, cf_p, sf_s, cf_s), wa, wg_, buf, cw, cb,
                                     tm=tm, tn=512, rows_per_group=lp, name="ffn_up")
        (wdn,) = _narrow(ffn_wdown, l, 1, rows=d_ff // 8)
        xp, xs = lin(u, u_s, wdn[None], 0, tn=512, out_dtype=F32, epi="residual",
                     epi_p=(xp, gf_p), epi_s=(xs, gf_s), name="ffn_down")
        last = tails.reshape(bp, lp // tm, SUBLANES, d_ff)[:, -1, SUBLANES - (CONV_W - 1):, :]
        conv_p.append(last)
        conv_s.append(jnp.concatenate([state_conv[l][:, 1:], a_s[:, None, :]], axis=1))

    y_prompt = _final_norm(xp, norm_final, tm).reshape(bp, lp, d)
    y_sample = _final_norm(xs, norm_final, bs).reshape(bs, ls, d)
    return (y_prompt, y_sample, jnp.stack(ret_p), jnp.stack(ret_s), jnp.stack(conv_p), jnp.stack(conv_s),
            k_p.reshape(bp, lp, nh_d, 2, dh_d), v_p.reshape(bp, lp, nh_d, dv_d),
            k_s.reshape(bs, ls, nh_d, 2, dh_d), v_s.reshape(bs, ls, nh_d, dv_d))
```
